```python
import jax, jax.numpy as jnp
from jax import lax
import numpy as np

D_MODEL = 1024
BATCH = 16
SEQ = 2048
DEPTH = 1

HEAD_DIM = 64
N_HEADS_A = 8
N_HEADS_B = 8
D_A = N_HEADS_A * HEAD_DIM
D_B = N_HEADS_B * HEAD_DIM
D_MIX = D_A + D_B
DILATED_PATTERNS = ((128, 1), (512, 4), (2048, 16))
MOBA_BLOCK = 256
MOBA_TOPK = 3
MOBA_Q_CHUNK = 128
ROPE_THETA = 10000.0
LN_EPS = 1e-5
NEG_INF = -1e30
DEEPNORM_ALPHA = (2.0 * DEPTH) ** 0.25
DEEPNORM_BETA = (8.0 * DEPTH) ** -0.25

kernel_name = "hybrid_dilated_moba_gated_deepnorm"


def _layernorm(x, g, b):
    xf = x.astype(jnp.float32)
    mu = xf.mean(-1, keepdims=True)
    var = jnp.square(xf - mu).mean(-1, keepdims=True)
    return ((xf - mu) * lax.rsqrt(var + LN_EPS) * g + b).astype(x.dtype)


def _rope(t, pos):
    half = HEAD_DIM // 2
    inv = ROPE_THETA ** (-jnp.arange(half, dtype=jnp.float32) / half)
    ang = pos.astype(jnp.float32)[:, None] * inv[None, :]
    cos, sin = jnp.cos(ang), jnp.sin(ang)
    t1, t2 = t[..., :half], t[..., half:]
    return jnp.concatenate([t1 * cos - t2 * sin, t1 * sin + t2 * cos], axis=-1).astype(t.dtype)


def _split_heads(t, n_heads):
    B, S, _ = t.shape
    return t.reshape(B, S, n_heads, HEAD_DIM).transpose(0, 2, 1, 3)


def _merge_heads(t):
    B, H, S, D = t.shape
    return t.transpose(0, 2, 1, 3).reshape(B, S, H * D)


def _dilated_window_attention(q, k, v, window, dilation):
    B, H, S, D = q.shape
    d = dilation
    w = window // d
    L = -(-S // d)
    Sp = L * d
    nb = -(-L // w)
    Lp = nb * w

    def to_classes(t):
        t = jnp.pad(t, ((0, 0), (0, 0), (0, Sp - S), (0, 0)))
        return t.reshape(B, H, L, d, D).transpose(0, 1, 3, 2, 4)

    qc, kc, vc = to_classes(q), to_classes(k), to_classes(v)
    qb = jnp.pad(qc, ((0, 0), (0, 0), (0, 0), (0, Lp - L), (0, 0))).reshape(B, H, d, nb, w, D)

    def key_bands(t):
        t = jnp.pad(t, ((0, 0), (0, 0), (0, 0), (w, Lp - L), (0, 0))).reshape(B, H, d, nb + 1, w, D)
        return jnp.concatenate([t[:, :, :, :-1], t[:, :, :, 1:]], axis=4)

    kb, vb = key_bands(kc), key_bands(vc)
    s = jnp.einsum('bhrnqd,bhrnkd->bhrnqk', qb, kb,
                   preferred_element_type=jnp.float32) * (HEAD_DIM ** -0.5)
    i = jnp.arange(w)[None, :, None]
    j = jnp.arange(2 * w)[None, None, :]
    n = jnp.arange(nb)[:, None, None]
    dist = i + w - j
    key_idx = n * w + j - w
    valid = (dist >= 0) & (dist <= w) & (key_idx >= 0)
    s = jnp.where(valid, s, NEG_INF)
    m = s.max(-1, keepdims=True)
    p = jnp.exp(s - m)
    l = p.sum(-1, keepdims=True)
    o = jnp.einsum('bhrnqk,bhrnkd->bhrnqd', (p / l).astype(v.dtype), vb)
    lse = (m + jnp.log(l))[..., 0]
    o = o.reshape(B, H, d, Lp, D)[:, :, :, :L].transpose(0, 1, 3, 2, 4).reshape(B, H, Sp, D)[:, :, :S]
    lse = lse.reshape(B, H, d, Lp)[..., :L].transpose(0, 1, 3, 2).reshape(B, H, Sp)[:, :, :S]
    return o, lse


def _dilated_mixture(q, k, v):
    outs, lses = [], []
    for window, dilation in DILATED_PATTERNS:
        o, lse = _dilated_window_attention(q, k, v, window, dilation)
        outs.append(o)
        lses.append(lse)
    wts = jax.nn.softmax(jnp.stack(lses, axis=0), axis=0)
    o = jnp.stack(outs, axis=0).astype(jnp.float32)
    return jnp.sum(wts[..., None] * o, axis=0).astype(q.dtype)


def _moba_attention(q, k, v):
    B, H, S, D = q.shape
    BS, QC = MOBA_BLOCK, MOBA_Q_CHUNK
    nb = -(-S // BS)
    Sp = nb * BS
    nq = Sp // QC
    topk = min(MOBA_TOPK, nb)
    pad = ((0, 0), (0, 0), (0, Sp - S), (0, 0))
    q, k, v = jnp.pad(q, pad), jnp.pad(k, pad), jnp.pad(v, pad)
    kb = k.reshape(B, H, nb, BS, D)
    vb = v.reshape(B, H, nb, BS, D)
    k_mean = kb.astype(jnp.float32).mean(axis=3)
    gate = jnp.einsum('bhsd,bhnd->bhsn', q.astype(jnp.float32), k_mean)
    own = jnp.arange(Sp) // BS
    past = jnp.arange(nb)[None, :] < own[:, None]
    gate = jnp.where(past, gate, NEG_INF)
    _, sel = lax.top_k(gate, topk)

    q_c = q.reshape(B, H, nq, QC, D).transpose(0, 2, 1, 3, 4).reshape(B * nq, H, QC, D)
    sel_c = sel.reshape(B, H, nq, QC, topk).transpose(0, 2, 1, 3, 4).reshape(B * nq, H, QC, topk)
    b_ids = jnp.repeat(jnp.arange(B, dtype=jnp.int32), nq)
    c_ids = jnp.tile(jnp.arange(nq, dtype=jnp.int32), B)
    h_ids = jnp.arange(H)[:, None, None]
    scale = HEAD_DIM ** -0.5

    def chunk(args):
        b_idx, c_idx, q_i, sel_i = args
        kb_b, vb_b = kb[b_idx], vb[b_idx]
        own_blk = (c_idx * QC) // BS
        k_sel = kb_b[h_ids, sel_i]
        v_sel = vb_b[h_ids, sel_i]
        k_own = lax.dynamic_index_in_dim(kb_b, own_blk, axis=1, keepdims=False)
        v_own = lax.dynamic_index_in_dim(vb_b, own_blk, axis=1, keepdims=False)
        s_sel = jnp.einsum('hqd,hqtkd->hqtk', q_i, k_sel, preferred_element_type=jnp.float32) * scale
        s_sel = jnp.where((sel_i < own_blk)[..., None], s_sel, NEG_INF).reshape(H, QC, topk * BS)
        s_own = jnp.einsum('hqd,hkd->hqk', q_i, k_own, preferred_element_type=jnp.float32) * scale
        qpos = c_idx * QC + jnp.arange(QC)
        kpos = own_blk * BS + jnp.arange(BS)
        s_own = jnp.where(kpos[None, :] <= qpos[:, None], s_own, NEG_INF)
        p = jax.nn.softmax(jnp.concatenate([s_sel, s_own], axis=-1), axis=-1).astype(v.dtype)
        p_sel = p[..., :topk * BS].reshape(H, QC, topk, BS)
        p_own = p[..., topk * BS:]
        return (jnp.einsum('hqtk,hqtkd->hqd', p_sel, v_sel)
                + jnp.einsum('hqk,hkd->hqd', p_own, v_own))

    o = lax.map(chunk, (b_ids, c_ids, q_c, sel_c))
    o = o.reshape(B, nq, H, QC, D).transpose(0, 2, 1, 3, 4).reshape(B, H, Sp, D)
    return o[:, :, :S]


def setup_inputs(seed: int = 0) -> dict:
    key = jax.random.key(seed)
    ks = jax.random.split(key, 8)
    x = jax.random.normal(ks[0], (BATCH, SEQ, D_MODEL), jnp.float32)
    c = jax.random.normal(ks[1], (BATCH, D_MODEL), jnp.float32)
    col_scale = jnp.concatenate([
        jnp.ones((3 * D_A,), jnp.float32).at[2 * D_A:].set(DEEPNORM_BETA),
        jnp.ones((D_A,), jnp.float32),
        jnp.ones((3 * D_B,), jnp.float32).at[2 * D_B:].set(DEEPNORM_BETA),
        jnp.ones((D_B,), jnp.float32)])
    w_in = jax.random.normal(ks[2], (DEPTH, D_MODEL, 4 * D_MIX), jnp.float32) * (D_MODEL ** -0.5) * col_scale
    w_out = jax.random.normal(ks[3], (DEPTH, D_MIX, D_MODEL), jnp.float32) * (D_MIX ** -0.5) * DEEPNORM_BETA
    w_ada = jax.random.normal(ks[4], (DEPTH, D_MODEL, 3 * D_MODEL), jnp.float32) * (0.5 * D_MODEL ** -0.5)
    b_ada = 0.02 * jax.random.normal(ks[5], (DEPTH, 3 * D_MODEL), jnp.float32)
    ln_g = 1.0 + 0.02 * jax.random.normal(ks[6], (DEPTH, D_MODEL), jnp.float32)
    ln_b = 0.02 * jax.random.normal(ks[7], (DEPTH, D_MODEL), jnp.float32)
    return {"x": x, "c": c, "w_in": w_in, "w_out": w_out, "w_ada": w_ada,
            "b_ada": b_ada, "ln_g": ln_g, "ln_b": ln_b}


def reference(x, c, w_in, w_out, w_ada, b_ada, ln_g, ln_b):
    B, S, _ = x.shape
    pos = jnp.arange(S, dtype=jnp.int32)
    offs = np.cumsum([0, D_A, D_A, D_A, D_A, D_B, D_B, D_B])[1:].tolist()
    for layer in range(DEPTH):
        mod = c @ w_ada[layer] + b_ada[layer]
        shift, scale, gate = jnp.split(mod, 3, axis=-1)
        h = x * (1.0 + scale[:, None, :]) + shift[:, None, :]
        proj = h @ w_in[layer]
        qa, ka, va, ga, qb, kb, vb, gb = jnp.split(proj, offs, axis=-1)
        qa, ka, va = _split_heads(qa, N_HEADS_A), _split_heads(ka, N_HEADS_A), _split_heads(va, N_HEADS_A)
        qb, kb, vb = _split_heads(qb, N_HEADS_B), _split_heads(kb, N_HEADS_B), _split_heads(vb, N_HEADS_B)
        qa, ka, qb, kb = _rope(qa, pos), _rope(ka, pos), _rope(qb, pos), _rope(kb, pos)
        o_a = _merge_heads(_dilated_mixture(qa, ka, va)) * jax.nn.silu(ga)
        o_b = _merge_heads(_moba_attention(qb, kb, vb)) * jax.nn.silu(gb)
        y = jnp.concatenate([o_a, o_b], axis=-1) @ w_out[layer]
        x = _layernorm(DEEPNORM_ALPHA * x + gate[:, None, :] * y, ln_g[layer], ln_b[layer])
    return x
```

```python
import functools

import numpy as np
import jax
import jax.numpy as jnp
from jax import lax
from jax.experimental import pallas as pl
from jax.experimental.pallas import tpu as pltpu

HEAD_DIM = 64
N_HEADS = 8
D_BRANCH = N_HEADS * HEAD_DIM
DILATED_PATTERNS = ((128, 1), (512, 4), (2048, 16))
MOBA_BLOCK = 256
MOBA_TOPK = 3
ROPE_THETA = 10000.0
LN_EPS = 1e-5
NEG_INF = -1e30
DEPTH = 1
DEEPNORM_ALPHA = (2.0 * DEPTH) ** 0.25

LANES = 128
HEADS_PER_GROUP = LANES // HEAD_DIM
N_GROUPS = N_HEADS // HEADS_PER_GROUP
ATT_TILE = 256
PROJ_ROWS = 512
VMEM_LIMIT = 48 * 1024 * 1024

NT_DIMS = (((1,), (1,)), ((), ()))

f32 = jnp.float32
bf16 = jnp.bfloat16


def _mod_kernel(c_ref, w_ref, b_ref, o_ref):
    o_ref[...] = jnp.dot(c_ref[...], w_ref[...], preferred_element_type=f32,
                         precision=lax.Precision.HIGHEST) + b_ref[...]


def _modulation(c, w_ada, b_ada):
    B, D = c.shape
    N = w_ada.shape[1]
    bn = 512
    return pl.pallas_call(
        _mod_kernel,
        grid=(N // bn,),
        in_specs=[pl.BlockSpec((B, D), lambda n: (0, 0)),
                  pl.BlockSpec((D, bn), lambda n: (0, n)),
                  pl.BlockSpec((1, bn), lambda n: (0, n))],
        out_specs=pl.BlockSpec((B, bn), lambda n: (0, n)),
        out_shape=jax.ShapeDtypeStruct((B, N), f32),
        name="adaln_mod",
    )(c, w_ada, b_ada.reshape(1, N))


def _proj_kernel(x_ref, mod_ref, cos_ref, sin_ref, w_ref,
                 qa_ref, ka_ref, va_ref, qb_ref, kb_ref, vb_ref, g_ref):
    tm = x_ref.shape[1]
    shift = mod_ref[0, 0:1, :]
    scale = mod_ref[0, 1:2, :]
    h = (x_ref[0] * (1.0 + scale) + shift).astype(bf16)
    cos = cos_ref[...]
    sin = sin_ref[...]
    lane = lax.broadcasted_iota(jnp.int32, (tm, LANES), 1)
    first_half = (lane & (HEAD_DIM // 2)) == 0

    def rope(t):
        partner = jnp.where(first_half,
                            pltpu.roll(t, LANES - HEAD_DIM // 2, 1),
                            pltpu.roll(t, HEAD_DIM // 2, 1))
        return t * cos + partner * sin

    targets = (qa_ref, ka_ref, va_ref, None, qb_ref, kb_ref, vb_ref, None)
    kinds = ("q", "k", "v", "g", "q", "k", "v", "g")
    for c in range(8):
        acc = jnp.dot(h, w_ref[:, c * D_BRANCH:(c + 1) * D_BRANCH],
                      preferred_element_type=f32)
        if kinds[c] == "g":
            off = 0 if c == 3 else D_BRANCH
            g_ref[0, :, off:off + D_BRANCH] = acc.astype(bf16)
            continue
        for gi in range(N_GROUPS):
            t = acc[:, gi * LANES:(gi + 1) * LANES]
            if kinds[c] == "q":
                t = rope(t) * (HEAD_DIM ** -0.5)
            elif kinds[c] == "k":
                t = rope(t)
            targets[c][0, gi] = t.astype(bf16)


def _projection(x, mod3, cos_t, sin_t, w_in_bf):
    B, S, D = x.shape
    tm = PROJ_ROWS
    n_s = S // tm
    qkv_shape = jax.ShapeDtypeStruct((B, N_GROUPS, S, LANES), bf16)
    qkv_spec = pl.BlockSpec((1, N_GROUPS, tm, LANES), lambda b, s: (b, 0, s, 0))
    return pl.pallas_call(
        _proj_kernel,
        grid=(B, n_s),
        in_specs=[pl.BlockSpec((1, tm, D), lambda b, s: (b, s, 0)),
                  pl.BlockSpec((1, 3, D), lambda b, s: (b, 0, 0)),
                  pl.BlockSpec((tm, LANES), lambda b, s: (s, 0)),
                  pl.BlockSpec((tm, LANES), lambda b, s: (s, 0)),
                  pl.BlockSpec(w_in_bf.shape, lambda b, s: (0, 0))],
        out_specs=[qkv_spec] * 6 + [pl.BlockSpec((1, tm, 2 * D_BRANCH), lambda b, s: (b, s, 0))],
        out_shape=[qkv_shape] * 6 + [jax.ShapeDtypeStruct((B, S, 2 * D_BRANCH), bf16)],
        compiler_params=pltpu.CompilerParams(
            dimension_semantics=("arbitrary", "arbitrary"), vmem_limit_bytes=VMEM_LIMIT),
        name="in_proj_rope",
    )(x, mod3, cos_t, sin_t, w_in_bf)


def _softmax_tile(first, h, qh, k, v, bias, m_ref, l_ref, acc_ref):
    s = lax.dot_general(qh, k, NT_DIMS, preferred_element_type=f32) + bias
    reps = s.shape[1] // LANES
    m_curr = jnp.max(s, axis=1)[:, None]
    if first:
        m_next = jnp.broadcast_to(m_curr, (s.shape[0], LANES))
    else:
        m_prev = m_ref[h]
        m_next = jnp.maximum(m_prev, m_curr)
    p = jnp.exp(s - jnp.tile(m_next, (1, reps)))
    l_curr = jnp.sum(p, axis=1)[:, None]
    pv = jnp.dot(p.astype(bf16), v, preferred_element_type=f32)
    if first:
        l_ref[h] = jnp.broadcast_to(l_curr, m_next.shape)
        acc_ref[h] = pv
    else:
        alpha = jnp.exp(m_prev - m_next)
        l_ref[h] = alpha * l_ref[h] + l_curr
        acc_ref[h] = alpha * acc_ref[h] + pv
    m_ref[h] = m_next


def _finish_tile(o_ref, rows, l_ref, acc_ref, lane_lo):
    out = jnp.where(lane_lo, acc_ref[0] / l_ref[0], acc_ref[1] / l_ref[1])
    o_ref[0, 0, rows, :] = out.astype(o_ref.dtype)


def _dilated_kernel(q_ref, k_ref, v_ref, bias_ref, o_ref, m_ref, l_ref, acc_ref):
    T = ATT_TILE
    n_t = q_ref.shape[2] // T
    lane_lo = lax.broadcasted_iota(jnp.int32, (T, LANES), 1) < HEAD_DIM

    def q_tile(i, carry):
        rows = pl.ds(pl.multiple_of(i * T, T), T)
        q = q_ref[0, 0, rows, :]
        zero = jnp.zeros_like(q)
        qh = (jnp.where(lane_lo, q, zero), jnp.where(lane_lo, zero, q))

        def kv_tile(j, first):
            cols = pl.ds(pl.multiple_of(j * T, T), T)
            k = k_ref[0, 0, cols, :]
            v = v_ref[0, 0, cols, :]
            bias = bias_ref[i - j]
            for h in range(HEADS_PER_GROUP):
                _softmax_tile(first, h, qh[h], k, v, bias, m_ref, l_ref, acc_ref)

        kv_tile(i, True)

        def body(j, c):
            kv_tile(j, False)
            return c
        lax.fori_loop(0, i, body, 0)
        _finish_tile(o_ref, rows, l_ref, acc_ref, lane_lo)
        return carry

    lax.fori_loop(0, n_t, q_tile, 0)


def _moba_kernel(q_ref, k_ref, v_ref, eye_ref, o_ref,
                 m_ref, l_ref, acc_ref, kmean_ref, bias_ref):
    T = ATT_TILE
    n_t = q_ref.shape[2] // T
    lane_lo = lax.broadcasted_iota(jnp.int32, (T, LANES), 1) < HEAD_DIM
    row_i = lax.broadcasted_iota(jnp.int32, (T, T), 0)
    col_i = lax.broadcasted_iota(jnp.int32, (T, T), 1)
    causal_bias = jnp.where(col_i <= row_i, 0.0, NEG_INF).astype(f32)

    kmean_ref[...] = jnp.zeros_like(kmean_ref)
    for j in range(n_t):
        kj = k_ref[0, 0, j * T:(j + 1) * T, :].astype(f32)
        kmean_ref[j:j + 1, :] = jnp.sum(kj, axis=0, keepdims=True) * (1.0 / T)
    kmean = kmean_ref[...]
    km_hi = kmean.astype(bf16)
    km_lo = (kmean - km_hi.astype(f32)).astype(bf16)

    blk = lax.broadcasted_iota(jnp.int32, (8, T), 0)

    def q_tile(i, carry):
        rows = pl.ds(pl.multiple_of(i * T, T), T)
        q = q_ref[0, 0, rows, :]
        zero = jnp.zeros_like(q)
        qh = (jnp.where(lane_lo, q, zero), jnp.where(lane_lo, zero, q))

        for h in range(HEADS_PER_GROUP):
            gate = (lax.dot_general(km_hi, qh[h], NT_DIMS, preferred_element_type=f32)
                    + lax.dot_general(km_lo, qh[h], NT_DIMS, preferred_element_type=f32))[0:8]
            cnt = jnp.zeros((8, T), f32)
            for jp in range(n_t):
                gj = gate[jp:jp + 1, :]
                ahead = (gj > gate) | ((gj == gate) & (jp < blk))
                cnt = cnt + jnp.where(ahead, 1.0, 0.0) * jnp.where(jp < i, 1.0, 0.0)
            sel = jnp.where((blk < i) & (cnt < float(MOBA_TOPK)), 1.0, 0.0)
            sel_pad = jnp.concatenate([sel, jnp.zeros((LANES - 8, T), f32)], axis=0).astype(bf16)
            sel_rows = lax.dot_general(eye_ref[...], sel_pad, NT_DIMS,
                                       preferred_element_type=f32)
            for j in range(n_t):
                flag = sel_rows[:, j:j + 1]
                bias_ref[h, j] = jnp.broadcast_to(jnp.where(flag > 0.5, 0.0, NEG_INF), (T, LANES))

        def kv_tile(j, first):
            cols = pl.ds(pl.multiple_of(j * T, T), T)
            k = k_ref[0, 0, cols, :]
            v = v_ref[0, 0, cols, :]
            for h in range(HEADS_PER_GROUP):
                if first:
                    bias = causal_bias
                else:
                    bias = jnp.tile(bias_ref[h, j], (1, T // LANES))
                _softmax_tile(first, h, qh[h], k, v, bias, m_ref, l_ref, acc_ref)

        kv_tile(i, True)

        def body(j, c):
            kv_tile(j, False)
            return c
        lax.fori_loop(0, i, body, 0)
        _finish_tile(o_ref, rows, l_ref, acc_ref, lane_lo)
        return carry

    lax.fori_loop(0, n_t, q_tile, 0)


def _dilated_bias_table(n_t):
    T = ATT_TILE
    r = np.arange(T)[:, None]
    c = np.arange(T)[None, :]
    table = np.empty((n_t, T, T), np.float32)
    for dt in range(n_t):
        dist = dt * T + r - c
        mult = np.zeros((T, T), np.float64)
        for window, dil in DILATED_PATTERNS:
            mult += (dist >= 0) & (dist <= window) & (dist % dil == 0)
        table[dt] = np.where(mult > 0, np.log(np.maximum(mult, 1.0)), NEG_INF)
    return jnp.asarray(table)


def _attention_specs(B, S):
    blk = pl.BlockSpec((1, 1, S, LANES), lambda b, g: (b, g, 0, 0))
    stats = [pltpu.VMEM((HEADS_PER_GROUP, ATT_TILE, LANES), f32)] * 3
    return blk, stats


def _dilated_attention(q, k, v):
    B, G, S, _ = q.shape
    n_t = S // ATT_TILE
    blk, stats = _attention_specs(B, S)
    table = _dilated_bias_table(n_t)
    return pl.pallas_call(
        _dilated_kernel,
        grid=(B, G),
        in_specs=[blk, blk, blk, pl.BlockSpec(table.shape, lambda b, g: (0, 0, 0))],
        out_specs=blk,
        out_shape=jax.ShapeDtypeStruct(q.shape, bf16),
        scratch_shapes=stats,
        compiler_params=pltpu.CompilerParams(
            dimension_semantics=("arbitrary", "arbitrary"), vmem_limit_bytes=VMEM_LIMIT),
        name="dilated_attn",
    )(q, k, v, table)


def _moba_attention(q, k, v):
    B, G, S, _ = q.shape
    n_t = S // ATT_TILE
    assert ATT_TILE == MOBA_BLOCK and n_t <= 8
    blk, stats = _attention_specs(B, S)
    eye = jnp.eye(ATT_TILE, dtype=bf16)
    return pl.pallas_call(
        _moba_kernel,
        grid=(B, G),
        in_specs=[blk, blk, blk, pl.BlockSpec(eye.shape, lambda b, g: (0, 0))],
        out_specs=blk,
        out_shape=jax.ShapeDtypeStruct(q.shape, bf16),
        scratch_shapes=stats + [pltpu.VMEM((16, LANES), f32),
                                pltpu.VMEM((HEADS_PER_GROUP, n_t, ATT_TILE, LANES), f32)],
        compiler_params=pltpu.CompilerParams(
            dimension_semantics=("arbitrary", "arbitrary"), vmem_limit_bytes=VMEM_LIMIT),
        name="moba_attn",
    )(q, k, v, eye)


def _out_kernel(oa_ref, ob_ref, g_ref, x_ref, mod_ref, w_ref, lng_ref, lnb_ref, o_ref, og_ref):
    for gi in range(N_GROUPS):
        for br, src in enumerate((oa_ref, ob_ref)):
            lo = br * D_BRANCH + gi * LANES
            g = g_ref[0, :, lo:lo + LANES].astype(f32)
            silu = g / (1.0 + jnp.exp(-g))
            og_ref[:, lo:lo + LANES] = (src[0, gi].astype(f32) * silu).astype(bf16)
    y = jnp.dot(og_ref[...], w_ref[...], preferred_element_type=f32)
    gate = mod_ref[0, 2:3, :]
    z = DEEPNORM_ALPHA * x_ref[0] + gate * y
    mu = jnp.mean(z, axis=-1, keepdims=True)
    zc = z - mu
    var = jnp.mean(zc * zc, axis=-1, keepdims=True)
    o_ref[0] = zc * lax.rsqrt(var + LN_EPS) * lng_ref[...] + lnb_ref[...]


def _out_projection(o_a, o_b, g, x, mod3, w_out_bf, ln_g, ln_b):
    B, S, D = x.shape
    tm = PROJ_ROWS
    att_spec = pl.BlockSpec((1, N_GROUPS, tm, LANES), lambda b, s: (b, 0, s, 0))
    row_spec = pl.BlockSpec((1, tm, D), lambda b, s: (b, s, 0))
    vec_spec = pl.BlockSpec((1, D), lambda b, s: (0, 0))
    return pl.pallas_call(
        _out_kernel,
        grid=(B, S // tm),
        in_specs=[att_spec, att_spec,
                  pl.BlockSpec((1, tm, 2 * D_BRANCH), lambda b, s: (b, s, 0)),
                  row_spec,
                  pl.BlockSpec((1, 3, D), lambda b, s: (b, 0, 0)),
                  pl.BlockSpec(w_out_bf.shape, lambda b, s: (0, 0)),
                  vec_spec, vec_spec],
        out_specs=row_spec,
        out_shape=jax.ShapeDtypeStruct((B, S, D), f32),
        scratch_shapes=[pltpu.VMEM((tm, 2 * D_BRANCH), bf16)],
        compiler_params=pltpu.CompilerParams(
            dimension_semantics=("arbitrary", "arbitrary"), vmem_limit_bytes=VMEM_LIMIT),
        name="out_proj_ln",
    )(o_a, o_b, g, x, mod3, w_out_bf, ln_g.reshape(1, D), ln_b.reshape(1, D))


def _rope_tables(S):
    half = HEAD_DIM // 2
    inv = ROPE_THETA ** (-jnp.arange(half, dtype=f32) / half)
    ang = jnp.arange(S, dtype=jnp.int32).astype(f32)[:, None] * inv[None, :]
    cos, sin = jnp.cos(ang), jnp.sin(ang)
    cos_head = jnp.concatenate([cos, cos], axis=-1)
    sin_head = jnp.concatenate([-sin, sin], axis=-1)
    return (jnp.tile(cos_head, (1, HEADS_PER_GROUP)), jnp.tile(sin_head, (1, HEADS_PER_GROUP)))


def kernel(x, c, w_in, w_out, w_ada, b_ada, ln_g, ln_b):
    B, S, D = x.shape
    cos_t, sin_t = _rope_tables(S)
    for layer in range(w_in.shape[0]):
        mod3 = _modulation(c, w_ada[layer], b_ada[layer]).reshape(B, 3, D)
        qa, ka, va, qb, kb, vb, g = _projection(x, mod3, cos_t, sin_t, w_in[layer].astype(bf16))
        o_a = _dilated_attention(qa, ka, va)
        o_b = _moba_attention(qb, kb, vb)
        x = _out_projection(o_a, o_b, g, x, mod3, w_out[layer].astype(bf16), ln_g[layer], ln_b[layer])
    return x
```

```python
import numpy as np
import jax
import jax.numpy as jnp
from jax import lax
from jax.experimental import pallas as pl
from jax.experimental.pallas import tpu as pltpu

HEAD_DIM = 64
N_HEADS = 8
D_BRANCH = N_HEADS * HEAD_DIM
DILATED_PATTERNS = ((128, 1), (512, 4), (2048, 16))
MOBA_BLOCK = 256
MOBA_TOPK = 3
ROPE_THETA = 10000.0
LN_EPS = 1e-5
NEG_INF = -1e30
DEPTH = 1
DEEPNORM_ALPHA = (2.0 * DEPTH) ** 0.25

LANES = 128
HEADS_PER_GROUP = LANES // HEAD_DIM
N_GROUPS = N_HEADS // HEADS_PER_GROUP
ATT_TILE = 256
PROJ_ROWS = 512
VMEM_LIMIT = 48 * 1024 * 1024

NT_DIMS = (((1,), (1,)), ((), ()))

f32 = jnp.float32
bf16 = jnp.bfloat16


def _mod_kernel(c_ref, w_ref, b_ref, o_ref):
    o_ref[...] = jnp.dot(c_ref[...], w_ref[...], preferred_element_type=f32,
                         precision=lax.Precision.HIGHEST) + b_ref[...]


def _modulation(c, w_ada, b_ada):
    B, D = c.shape
    N = w_ada.shape[1]
    bn = 512
    return pl.pallas_call(
        _mod_kernel,
        grid=(N // bn,),
        in_specs=[pl.BlockSpec((B, D), lambda n: (0, 0)),
                  pl.BlockSpec((D, bn), lambda n: (0, n)),
                  pl.BlockSpec((1, bn), lambda n: (0, n))],
        out_specs=pl.BlockSpec((B, bn), lambda n: (0, n)),
        out_shape=jax.ShapeDtypeStruct((B, N), f32),
        name="adaln_mod",
    )(c, w_ada, b_ada.reshape(1, N))


def _proj_kernel(x_ref, mod_ref, cos_ref, sin_ref, w_ref,
                 qa_ref, ka_ref, va_ref, qb_ref, kb_ref, vb_ref, g_ref):
    tm = x_ref.shape[1]
    shift = mod_ref[0, 0:1, :]
    scale = mod_ref[0, 1:2, :]
    h = (x_ref[0] * (1.0 + scale) + shift).astype(bf16)
    cos = cos_ref[...]
    sin = sin_ref[...]
    lane = lax.broadcasted_iota(jnp.int32, (tm, LANES), 1)
    first_half = (lane & (HEAD_DIM // 2)) == 0

    def rope(t):
        partner = jnp.where(first_half,
                            pltpu.roll(t, LANES - HEAD_DIM // 2, 1),
                            pltpu.roll(t, HEAD_DIM // 2, 1))
        return t * cos + partner * sin

    targets = (qa_ref, ka_ref, va_ref, None, qb_ref, kb_ref, vb_ref, None)
    kinds = ("q", "k", "v", "g", "q", "k", "v", "g")
    for c in range(8):
        acc = jnp.dot(h, w_ref[:, c * D_BRANCH:(c + 1) * D_BRANCH],
                      preferred_element_type=f32)
        if kinds[c] == "g":
            off = 0 if c == 3 else D_BRANCH
            g_ref[0, :, off:off + D_BRANCH] = acc.astype(bf16)
            continue
        for gi in range(N_GROUPS):
            t = acc[:, gi * LANES:(gi + 1) * LANES]
            if kinds[c] == "q":
                t = rope(t) * (HEAD_DIM ** -0.5)
            elif kinds[c] == "k":
                t = rope(t)
            targets[c][0, gi] = t.astype(bf16)


def _projection(x, mod3, cos_t, sin_t, w_in_bf):
    B, S, D = x.shape
    tm = PROJ_ROWS
    n_s = S // tm
    qkv_shape = jax.ShapeDtypeStruct((B, N_GROUPS, S, LANES), bf16)
    qkv_spec = pl.BlockSpec((1, N_GROUPS, tm, LANES), lambda b, s: (b, 0, s, 0))
    return pl.pallas_call(
        _proj_kernel,
        grid=(B, n_s),
        in_specs=[pl.BlockSpec((1, tm, D), lambda b, s: (b, s, 0)),
                  pl.BlockSpec((1, 3, D), lambda b, s: (b, 0, 0)),
                  pl.BlockSpec((tm, LANES), lambda b, s: (s, 0)),
                  pl.BlockSpec((tm, LANES), lambda b, s: (s, 0)),
                  pl.BlockSpec(w_in_bf.shape, lambda b, s: (0, 0))],
        out_specs=[qkv_spec] * 6 + [pl.BlockSpec((1, tm, 2 * D_BRANCH), lambda b, s: (b, s, 0))],
        out_shape=[qkv_shape] * 6 + [jax.ShapeDtypeStruct((B, S, 2 * D_BRANCH), bf16)],
        compiler_params=pltpu.CompilerParams(
            dimension_semantics=("arbitrary", "arbitrary"), vmem_limit_bytes=VMEM_LIMIT),
        name="in_proj_rope",
    )(x, mod3, cos_t, sin_t, w_in_bf)


def _own_lanes(h, shape):
    lane = lax.broadcasted_iota(jnp.int32, shape, len(shape) - 1)
    return (lane < HEAD_DIM) if h == 0 else (lane >= HEAD_DIM)


def _fill_v_aug(v_ref, vaug_ref):
    v = v_ref[0, 0]
    one = jnp.ones_like(v)
    for h in range(HEADS_PER_GROUP):
        vaug_ref[h] = jnp.where(_own_lanes(h, v.shape), v, one)


def _two_pass_rows(i, h, q_aug, k_tile, bias_tile, vaug_ref, s_ref):
    T = ATT_TILE
    mrun = None
    for j in range(i + 1):
        s = lax.dot_general(q_aug, k_tile(j), NT_DIMS, preferred_element_type=f32)
        b = bias_tile(j)
        if b is not None:
            s = s + b
        s_ref[h, :, j * T:(j + 1) * T] = s
        t = jnp.maximum(s[:, :LANES], s[:, LANES:])
        mrun = t if mrun is None else jnp.maximum(mrun, t)
    m = jnp.max(mrun, axis=1, keepdims=True)
    acc = None
    for j in range(i + 1):
        p = jnp.exp(s_ref[h, :, j * T:(j + 1) * T] - m).astype(bf16)
        pv = jnp.dot(p, vaug_ref[h, j * T:(j + 1) * T, :], preferred_element_type=f32)
        acc = pv if acc is None else acc + pv
    return acc


def _store_rows(o_ref, i, accs):
    T = ATT_TILE
    outs = [a / pltpu.roll(a, HEAD_DIM, 1) for a in accs]
    out = jnp.where(_own_lanes(0, outs[0].shape), outs[0], outs[1])
    o_ref[0, 0, i * T:(i + 1) * T, :] = out.astype(o_ref.dtype)


def _dilated_kernel(q_ref, k_ref, v_ref, slab_ref, o_ref, s_ref, vaug_ref):
    T = ATT_TILE
    n_t = q_ref.shape[2] // T
    _fill_v_aug(v_ref, vaug_ref)
    for i in range(n_t):
        q = q_ref[0, 0, i * T:(i + 1) * T, :]
        accs = []
        for h in range(HEADS_PER_GROUP):
            qh = jnp.where(_own_lanes(h, q.shape), q, jnp.zeros_like(q))
            accs.append(_two_pass_rows(
                i, h, qh,
                lambda j: k_ref[0, 0, j * T:(j + 1) * T, :],
                lambda j, i=i: slab_ref[:, (n_t - 1 - i + j) * T:(n_t - i + j) * T],
                vaug_ref, s_ref))
        _store_rows(o_ref, i, accs)


def _moba_kernel(q_ref, k_ref, v_ref, eye_ref, o_ref, s_ref, vaug_ref, kaug_ref, kmean_ref):
    T = ATT_TILE
    S = q_ref.shape[2]
    n_t = S // T
    spare = (HEAD_DIM, 0)
    _fill_v_aug(v_ref, vaug_ref)

    kmean_ref[...] = jnp.zeros_like(kmean_ref)
    for j in range(n_t):
        kj = k_ref[0, 0, j * T:(j + 1) * T, :]
        kmean_ref[j:j + 1, :] = jnp.sum(kj.astype(f32), axis=0, keepdims=True) * (1.0 / T)
        lane = lax.broadcasted_iota(jnp.int32, kj.shape, 1)
        for h in range(HEADS_PER_GROUP):
            onehot = jnp.where(lane == spare[h] + j, 1.0, 0.0).astype(bf16)
            kaug_ref[h, j * T:(j + 1) * T, :] = jnp.where(_own_lanes(h, kj.shape), kj, onehot)
    kmean = kmean_ref[...]
    km_hi = kmean.astype(bf16)
    km_lo = (kmean - km_hi.astype(f32)).astype(bf16)

    row_i = lax.broadcasted_iota(jnp.int32, (T, T), 0)
    col_i = lax.broadcasted_iota(jnp.int32, (T, T), 1)
    causal_bias = jnp.where(col_i <= row_i, 0.0, NEG_INF).astype(f32)

    blk = lax.broadcasted_iota(jnp.int32, (8, S), 0)
    own_blk = lax.broadcasted_iota(jnp.int32, (8, S), 1) // T
    past = blk < own_blk
    q_all = q_ref[0, 0]
    keep_pads = []
    for h in range(HEADS_PER_GROUP):
        qh_all = jnp.where(_own_lanes(h, q_all.shape), q_all, jnp.zeros_like(q_all))
        gate = (lax.dot_general(km_hi, qh_all, NT_DIMS, preferred_element_type=f32)
                + lax.dot_general(km_lo, qh_all, NT_DIMS, preferred_element_type=f32))[0:8]
        cnt = jnp.zeros((8, S), f32)
        for jp in range(n_t):
            gj = gate[jp:jp + 1, :]
            ahead = (gj > gate) | ((gj == gate) & (jp < blk))
            cnt = cnt + jnp.where(ahead & (jp < own_blk), 1.0, 0.0)
        keep = jnp.where(past, jnp.where(cnt < float(MOBA_TOPK), 1.0, 0.0), 1.0)
        pad_rows = (spare[h], LANES - 8 - spare[h])
        parts = [jnp.zeros((n, S), f32) for n in pad_rows[:1] if n] + [keep] + \
                [jnp.zeros((n, S), f32) for n in pad_rows[1:] if n]
        keep_pad = jnp.concatenate(parts, axis=0).astype(bf16)
        keep_pads.append(keep_pad)

    for i in range(n_t):
        q = q_ref[0, 0, i * T:(i + 1) * T, :]
        accs = []
        for h in range(HEADS_PER_GROUP):
            keep_rows = lax.dot_general(eye_ref[...], keep_pads[h][:, i * T:(i + 1) * T], NT_DIMS,
                                        preferred_element_type=f32)
            bias_rows = jnp.where(keep_rows > 0.5, 0.0, NEG_INF).astype(bf16)
            q_aug = jnp.where(_own_lanes(h, q.shape), q, bias_rows)
            accs.append(_two_pass_rows(
                i, h, q_aug,
                lambda j, h=h: kaug_ref[h, j * T:(j + 1) * T, :],
                lambda j, i=i: causal_bias if j == i else None,
                vaug_ref, s_ref))
        _store_rows(o_ref, i, accs)


def _dilated_bias_slab(n_t):
    T = ATT_TILE
    r = np.arange(T)[:, None]
    c = np.arange(T)[None, :]
    slab = np.empty((T, n_t * T), np.float32)
    for b in range(n_t):
        dist = (n_t - 1 - b) * T + r - c
        mult = np.zeros((T, T), np.float64)
        for window, dil in DILATED_PATTERNS:
            mult += (dist >= 0) & (dist <= window) & (dist % dil == 0)
        slab[:, b * T:(b + 1) * T] = np.where(mult > 0, np.log(np.maximum(mult, 1.0)), NEG_INF)
    return jnp.asarray(slab)


def _attention_call(body, name, q, k, v, const, extra_scratch):
    B, G, S, _ = q.shape
    blk = pl.BlockSpec((1, 1, S, LANES), lambda b, g: (b, g, 0, 0))
    scratch = [pltpu.VMEM((HEADS_PER_GROUP, ATT_TILE, S), f32),
               pltpu.VMEM((HEADS_PER_GROUP, S, LANES), bf16)]
    return pl.pallas_call(
        body,
        grid=(B, G),
        in_specs=[blk, blk, blk, pl.BlockSpec(const.shape, lambda b, g: (0, 0))],
        out_specs=blk,
        out_shape=jax.ShapeDtypeStruct(q.shape, bf16),
        scratch_shapes=scratch + extra_scratch,
        compiler_params=pltpu.CompilerParams(
            dimension_semantics=("arbitrary", "arbitrary"), vmem_limit_bytes=VMEM_LIMIT),
        name=name,
    )(q, k, v, const)


def _dilated_attention(q, k, v):
    S = q.shape[2]
    return _attention_call(_dilated_kernel, "dilated_attn", q, k, v,
                           _dilated_bias_slab(S // ATT_TILE), [])


def _moba_attention(q, k, v):
    S = q.shape[2]
    assert ATT_TILE == MOBA_BLOCK and S // ATT_TILE <= 8
    return _attention_call(_moba_kernel, "moba_attn", q, k, v, jnp.eye(ATT_TILE, dtype=bf16),
                           [pltpu.VMEM((HEADS_PER_GROUP, S, LANES), bf16),
                            pltpu.VMEM((16, LANES), f32)])


def _out_kernel(oa_ref, ob_ref, g_ref, x_ref, mod_ref, w_ref, lng_ref, lnb_ref, o_ref, og_ref):
    for gi in range(N_GROUPS):
        for br, src in enumerate((oa_ref, ob_ref)):
            lo = br * D_BRANCH + gi * LANES
            g = g_ref[0, :, lo:lo + LANES].astype(f32)
            silu = g / (1.0 + jnp.exp(-g))
            og_ref[:, lo:lo + LANES] = (src[0, gi].astype(f32) * silu).astype(bf16)
    y = jnp.dot(og_ref[...], w_ref[...], preferred_element_type=f32)
    gate = mod_ref[0, 2:3, :]
    z = DEEPNORM_ALPHA * x_ref[0] + gate * y
    mu = jnp.mean(z, axis=-1, keepdims=True)
    zc = z - mu
    var = jnp.mean(zc * zc, axis=-1, keepdims=True)
    o_ref[0] = zc * lax.rsqrt(var + LN_EPS) * lng_ref[...] + lnb_ref[...]


def _out_projection(o_a, o_b, g, x, mod3, w_out_bf, ln_g, ln_b):
    B, S, D = x.shape
    tm = PROJ_ROWS
    att_spec = pl.BlockSpec((1, N_GROUPS, tm, LANES), lambda b, s: (b, 0, s, 0))
    row_spec = pl.BlockSpec((1, tm, D), lambda b, s: (b, s, 0))
    vec_spec = pl.BlockSpec((1, D), lambda b, s: (0, 0))
    return pl.pallas_call(
        _out_kernel,
        grid=(B, S // tm),
        in_specs=[att_spec, att_spec,
                  pl.BlockSpec((1, tm, 2 * D_BRANCH), lambda b, s: (b, s, 0)),
                  row_spec,
                  pl.BlockSpec((1, 3, D), lambda b, s: (b, 0, 0)),
                  pl.BlockSpec(w_out_bf.shape, lambda b, s: (0, 0)),
                  vec_spec, vec_spec],
        out_specs=row_spec,
        out_shape=jax.ShapeDtypeStruct((B, S, D), f32),
        scratch_shapes=[pltpu.VMEM((tm, 2 * D_BRANCH), bf16)],
        compiler_params=pltpu.CompilerParams(
            dimension_semantics=("arbitrary", "arbitrary"), vmem_limit_bytes=VMEM_LIMIT),
        name="out_proj_ln",
    )(o_a, o_b, g, x, mod3, w_out_bf, ln_g.reshape(1, D), ln_b.reshape(1, D))


def _rope_tables(S):
    half = HEAD_DIM // 2
    inv = ROPE_THETA ** (-jnp.arange(half, dtype=f32) / half)
    ang = jnp.arange(S, dtype=jnp.int32).astype(f32)[:, None] * inv[None, :]
    cos, sin = jnp.cos(ang), jnp.sin(ang)
    cos_head = jnp.concatenate([cos, cos], axis=-1)
    sin_head = jnp.concatenate([-sin, sin], axis=-1)
    return (jnp.tile(cos_head, (1, HEADS_PER_GROUP)), jnp.tile(sin_head, (1, HEADS_PER_GROUP)))


def kernel(x, c, w_in, w_out, w_ada, b_ada, ln_g, ln_b):
    B, S, D = x.shape
    cos_t, sin_t = _rope_tables(S)
    for layer in range(w_in.shape[0]):
        mod3 = _modulation(c, w_ada[layer], b_ada[layer]).reshape(B, 3, D)
        qa, ka, va, qb, kb, vb, g = _projection(x, mod3, cos_t, sin_t, w_in[layer].astype(bf16))
        o_a = _dilated_attention(qa, ka, va)
        o_b = _moba_attention(qb, kb, vb)
        x = _out_projection(o_a, o_b, g, x, mod3, w_out[layer].astype(bf16), ln_g[layer], ln_b[layer])
    return x
```

```python
import numpy as np
import jax
import jax.numpy as jnp
from jax import lax
from jax.experimental import pallas as pl
from jax.experimental.pallas import tpu as pltpu

HEAD_DIM = 64
N_HEADS = 8
D_BRANCH = N_HEADS * HEAD_DIM
DILATED_PATTERNS = ((128, 1), (512, 4), (2048, 16))
MOBA_BLOCK = 256
MOBA_TOPK = 3
ROPE_THETA = 10000.0
LN_EPS = 1e-5
NEG_INF = -1e30
DEPTH = 1
DEEPNORM_ALPHA = (2.0 * DEPTH) ** 0.25

LANES = 128
HEADS_PER_GROUP = LANES // HEAD_DIM
N_GROUPS = N_HEADS // HEADS_PER_GROUP
ATT_TILE = 256
PROJ_ROWS = 512
VMEM_LIMIT = 48 * 1024 * 1024

NT_DIMS = (((1,), (1,)), ((), ()))

f32 = jnp.float32
bf16 = jnp.bfloat16


def _mod_kernel(c_ref, w_ref, b_ref, o_ref):
    o_ref[...] = jnp.dot(c_ref[...], w_ref[...], preferred_element_type=f32,
                         precision=lax.Precision.HIGHEST) + b_ref[...]


def _modulation(c, w_ada, b_ada):
    B, D = c.shape
    N = w_ada.shape[1]
    bn = 512
    return pl.pallas_call(
        _mod_kernel,
        grid=(N // bn,),
        in_specs=[pl.BlockSpec((B, D), lambda n: (0, 0)),
                  pl.BlockSpec((D, bn), lambda n: (0, n)),
                  pl.BlockSpec((1, bn), lambda n: (0, n))],
        out_specs=pl.BlockSpec((B, bn), lambda n: (0, n)),
        out_shape=jax.ShapeDtypeStruct((B, N), f32),
        name="adaln_mod",
    )(c, w_ada, b_ada.reshape(1, N))


def _proj_kernel(x_ref, mod_ref, cos_ref, sin_ref, w_ref,
                 qa_ref, ka_ref, va_ref, qb_ref, kb_ref, vb_ref, g_ref):
    tm = x_ref.shape[1]
    shift = mod_ref[0, 0:1, :]
    scale = mod_ref[0, 1:2, :]
    h = (x_ref[0] * (1.0 + scale) + shift).astype(bf16)
    cos = cos_ref[...]
    sin = sin_ref[...]
    lane = lax.broadcasted_iota(jnp.int32, (tm, LANES), 1)
    first_half = (lane & (HEAD_DIM // 2)) == 0

    def rope(t):
        partner = jnp.where(first_half,
                            pltpu.roll(t, LANES - HEAD_DIM // 2, 1),
                            pltpu.roll(t, HEAD_DIM // 2, 1))
        return t * cos + partner * sin

    targets = (qa_ref, ka_ref, va_ref, None, qb_ref, kb_ref, vb_ref, None)
    kinds = ("q", "k", "v", "g", "q", "k", "v", "g")
    for c in range(8):
        acc = jnp.dot(h, w_ref[:, c * D_BRANCH:(c + 1) * D_BRANCH],
                      preferred_element_type=f32)
        if kinds[c] == "g":
            off = 0 if c == 3 else D_BRANCH
            g_ref[0, :, off:off + D_BRANCH] = acc.astype(bf16)
            continue
        for gi in range(N_GROUPS):
            t = acc[:, gi * LANES:(gi + 1) * LANES]
            if kinds[c] == "q":
                t = rope(t) * (HEAD_DIM ** -0.5)
            elif kinds[c] == "k":
                t = rope(t)
            targets[c][0, gi] = t.astype(bf16)


def _projection(x, mod3, cos_t, sin_t, w_in_bf):
    B, S, D = x.shape
    tm = PROJ_ROWS
    n_s = S // tm
    qkv_shape = jax.ShapeDtypeStruct((B, N_GROUPS, S, LANES), bf16)
    qkv_spec = pl.BlockSpec((1, N_GROUPS, tm, LANES), lambda b, s: (b, 0, s, 0))
    return pl.pallas_call(
        _proj_kernel,
        grid=(B, n_s),
        in_specs=[pl.BlockSpec((1, tm, D), lambda b, s: (b, s, 0)),
                  pl.BlockSpec((1, 3, D), lambda b, s: (b, 0, 0)),
                  pl.BlockSpec((tm, LANES), lambda b, s: (s, 0)),
                  pl.BlockSpec((tm, LANES), lambda b, s: (s, 0)),
                  pl.BlockSpec(w_in_bf.shape, lambda b, s: (0, 0))],
        out_specs=[qkv_spec] * 6 + [pl.BlockSpec((1, tm, 2 * D_BRANCH), lambda b, s: (b, s, 0))],
        out_shape=[qkv_shape] * 6 + [jax.ShapeDtypeStruct((B, S, 2 * D_BRANCH), bf16)],
        compiler_params=pltpu.CompilerParams(
            dimension_semantics=("arbitrary", "arbitrary"), vmem_limit_bytes=VMEM_LIMIT),
        name="in_proj_rope",
    )(x, mod3, cos_t, sin_t, w_in_bf)


def _own_lanes(h, shape):
    lane = lax.broadcasted_iota(jnp.int32, shape, len(shape) - 1)
    return (lane < HEAD_DIM) if h == 0 else (lane >= HEAD_DIM)


def _fill_v_aug(v_ref, vaug_ref):
    v = v_ref[0, 0]
    one = jnp.ones_like(v)
    for h in range(HEADS_PER_GROUP):
        vaug_ref[h] = jnp.where(_own_lanes(h, v.shape), v, one)


class _Rows:
    def __init__(self, i, h, make_q, k_tile, bias_tile):
        self.i, self.h = i, h
        self.make_q, self.k_tile, self.bias_tile = make_q, k_tile, bias_tile
        self.q_aug = self.mrun = self.m = self.acc = None


def _score_tile(u, j, s_ref, slot):
    T = ATT_TILE
    if u.q_aug is None:
        u.q_aug = u.make_q()
    s = lax.dot_general(u.q_aug, u.k_tile(j), NT_DIMS, preferred_element_type=f32)
    b = u.bias_tile(j)
    if b is not None:
        s = s + b
    s_ref[slot, :, j * T:(j + 1) * T] = s
    t = jnp.maximum(s[:, :LANES], s[:, LANES:])
    u.mrun = t if u.mrun is None else jnp.maximum(u.mrun, t)


def _value_tile(u, j, s_ref, slot, vaug_ref):
    T = ATT_TILE
    p = jnp.exp(s_ref[slot, :, j * T:(j + 1) * T] - u.m).astype(bf16)
    pv = jnp.dot(p, vaug_ref[u.h, j * T:(j + 1) * T, :], preferred_element_type=f32)
    u.acc = pv if u.acc is None else u.acc + pv


def _run_rows(units, s_ref, vaug_ref, o_ref):
    T = ATT_TILE
    done = {}
    prev = None
    for idx, u in enumerate(list(units) + [None]):
        n_a = u.i + 1 if u is not None else 0
        n_b = prev.i + 1 if prev is not None else 0
        for t in range(max(n_a, n_b)):
            if t < n_a:
                _score_tile(u, t, s_ref, idx % 2)
            if t < n_b:
                _value_tile(prev, t, s_ref, (idx - 1) % 2, vaug_ref)
        if u is not None:
            u.m = jnp.max(u.mrun, axis=1, keepdims=True)
        if prev is not None:
            done[prev.h] = prev.acc / pltpu.roll(prev.acc, HEAD_DIM, 1)
            if len(done) == HEADS_PER_GROUP:
                out = jnp.where(_own_lanes(0, done[0].shape), done[0], done[1])
                o_ref[0, 0, prev.i * T:(prev.i + 1) * T, :] = out.astype(o_ref.dtype)
                done = {}
        prev = u


def _dilated_kernel(q_ref, k_ref, v_ref, slab_ref, o_ref, s_ref, vaug_ref):
    T = ATT_TILE
    n_t = q_ref.shape[2] // T
    _fill_v_aug(v_ref, vaug_ref)

    def make_q(i, h):
        q = q_ref[0, 0, i * T:(i + 1) * T, :]
        return jnp.where(_own_lanes(h, q.shape), q, jnp.zeros_like(q))

    units = [_Rows(i, h,
                   lambda i=i, h=h: make_q(i, h),
                   lambda j: k_ref[0, 0, j * T:(j + 1) * T, :],
                   lambda j, i=i: slab_ref[:, (n_t - 1 - i + j) * T:(n_t - i + j) * T])
             for i in range(n_t) for h in range(HEADS_PER_GROUP)]
    _run_rows(units, s_ref, vaug_ref, o_ref)


def _moba_kernel(q_ref, k_ref, v_ref, eye_ref, o_ref, s_ref, vaug_ref, kaug_ref, kmean_ref):
    T = ATT_TILE
    S = q_ref.shape[2]
    n_t = S // T
    spare = (HEAD_DIM, 0)
    _fill_v_aug(v_ref, vaug_ref)

    kmean_ref[...] = jnp.zeros_like(kmean_ref)
    for j in range(n_t):
        kj = k_ref[0, 0, j * T:(j + 1) * T, :]
        kmean_ref[j:j + 1, :] = jnp.sum(kj.astype(f32), axis=0, keepdims=True) * (1.0 / T)
        lane = lax.broadcasted_iota(jnp.int32, kj.shape, 1)
        for h in range(HEADS_PER_GROUP):
            onehot = jnp.where(lane == spare[h] + j, 1.0, 0.0).astype(bf16)
            kaug_ref[h, j * T:(j + 1) * T, :] = jnp.where(_own_lanes(h, kj.shape), kj, onehot)
    kmean = kmean_ref[...]
    km_hi = kmean.astype(bf16)
    km_lo = (kmean - km_hi.astype(f32)).astype(bf16)

    row_i = lax.broadcasted_iota(jnp.int32, (T, T), 0)
    col_i = lax.broadcasted_iota(jnp.int32, (T, T), 1)
    causal_bias = jnp.where(col_i <= row_i, 0.0, NEG_INF).astype(f32)

    blk = lax.broadcasted_iota(jnp.int32, (8, S), 0)
    own_blk = lax.broadcasted_iota(jnp.int32, (8, S), 1) // T
    past = blk < own_blk
    q_all = q_ref[0, 0]
    keep_pads = []
    for h in range(HEADS_PER_GROUP):
        qh_all = jnp.where(_own_lanes(h, q_all.shape), q_all, jnp.zeros_like(q_all))
        gate = (lax.dot_general(km_hi, qh_all, NT_DIMS, preferred_element_type=f32)
                + lax.dot_general(km_lo, qh_all, NT_DIMS, preferred_element_type=f32))[0:8]
        cnt = jnp.zeros((8, S), f32)
        for jp in range(n_t):
            gj = gate[jp:jp + 1, :]
            ahead = (gj > gate) | ((gj == gate) & (jp < blk))
            cnt = cnt + jnp.where(ahead & (jp < own_blk), 1.0, 0.0)
        keep = jnp.where(past, jnp.where(cnt < float(MOBA_TOPK), 1.0, 0.0), 1.0)
        pad_rows = (spare[h], LANES - 8 - spare[h])
        parts = [jnp.zeros((n, S), f32) for n in pad_rows[:1] if n] + [keep] + \
                [jnp.zeros((n, S), f32) for n in pad_rows[1:] if n]
        keep_pad = jnp.concatenate(parts, axis=0).astype(bf16)
        keep_pads.append(keep_pad)

    def make_q(i, h):
        q = q_ref[0, 0, i * T:(i + 1) * T, :]
        keep_rows = lax.dot_general(eye_ref[...], keep_pads[h][:, i * T:(i + 1) * T], NT_DIMS,
                                    preferred_element_type=f32)
        bias_rows = jnp.where(keep_rows > 0.5, 0.0, NEG_INF).astype(bf16)
        return jnp.where(_own_lanes(h, q.shape), q, bias_rows)

    units = [_Rows(i, h,
                   lambda i=i, h=h: make_q(i, h),
                   lambda j, h=h: kaug_ref[h, j * T:(j + 1) * T, :],
                   lambda j, i=i: causal_bias if j == i else None)
             for i in range(n_t) for h in range(HEADS_PER_GROUP)]
    _run_rows(units, s_ref, vaug_ref, o_ref)


def _dilated_bias_slab(n_t):
    T = ATT_TILE
    r = np.arange(T)[:, None]
    c = np.arange(T)[None, :]
    slab = np.empty((T, n_t * T), np.float32)
    for b in range(n_t):
        dist = (n_t - 1 - b) * T + r - c
        mult = np.zeros((T, T), np.float64)
        for window, dil in DILATED_PATTERNS:
            mult += (dist >= 0) & (dist <= window) & (dist % dil == 0)
        slab[:, b * T:(b + 1) * T] = np.where(mult > 0, np.log(np.maximum(mult, 1.0)), NEG_INF)
    return jnp.asarray(slab)


def _attention_call(body, name, q, k, v, const, extra_scratch):
    B, G, S, _ = q.shape
    blk = pl.BlockSpec((1, 1, S, LANES), lambda b, g: (b, g, 0, 0))
    scratch = [pltpu.VMEM((HEADS_PER_GROUP, ATT_TILE, S), f32),
               pltpu.VMEM((HEADS_PER_GROUP, S, LANES), bf16)]
    return pl.pallas_call(
        body,
        grid=(B, G),
        in_specs=[blk, blk, blk, pl.BlockSpec(const.shape, lambda b, g: (0, 0))],
        out_specs=blk,
        out_shape=jax.ShapeDtypeStruct(q.shape, bf16),
        scratch_shapes=scratch + extra_scratch,
        compiler_params=pltpu.CompilerParams(
            dimension_semantics=("arbitrary", "arbitrary"), vmem_limit_bytes=VMEM_LIMIT),
        name=name,
    )(q, k, v, const)


def _dilated_attention(q, k, v):
    S = q.shape[2]
    return _attention_call(_dilated_kernel, "dilated_attn", q, k, v,
                           _dilated_bias_slab(S // ATT_TILE), [])


def _moba_attention(q, k, v):
    S = q.shape[2]
    assert ATT_TILE == MOBA_BLOCK and S // ATT_TILE <= 8
    return _attention_call(_moba_kernel, "moba_attn", q, k, v, jnp.eye(ATT_TILE, dtype=bf16),
                           [pltpu.VMEM((HEADS_PER_GROUP, S, LANES), bf16),
                            pltpu.VMEM((16, LANES), f32)])


def _out_kernel(oa_ref, ob_ref, g_ref, x_ref, mod_ref, w_ref, lng_ref, lnb_ref, o_ref, og_ref):
    for gi in range(N_GROUPS):
        for br, src in enumerate((oa_ref, ob_ref)):
            lo = br * D_BRANCH + gi * LANES
            g = g_ref[0, :, lo:lo + LANES].astype(f32)
            silu = g / (1.0 + jnp.exp(-g))
            og_ref[:, lo:lo + LANES] = (src[0, gi].astype(f32) * silu).astype(bf16)
    y = jnp.dot(og_ref[...], w_ref[...], preferred_element_type=f32)
    gate = mod_ref[0, 2:3, :]
    z = DEEPNORM_ALPHA * x_ref[0] + gate * y
    mu = jnp.mean(z, axis=-1, keepdims=True)
    zc = z - mu
    var = jnp.mean(zc * zc, axis=-1, keepdims=True)
    o_ref[0] = zc * lax.rsqrt(var + LN_EPS) * lng_ref[...] + lnb_ref[...]


def _out_projection(o_a, o_b, g, x, mod3, w_out_bf, ln_g, ln_b):
    B, S, D = x.shape
    tm = PROJ_ROWS
    att_spec = pl.BlockSpec((1, N_GROUPS, tm, LANES), lambda b, s: (b, 0, s, 0))
    row_spec = pl.BlockSpec((1, tm, D), lambda b, s: (b, s, 0))
    vec_spec = pl.BlockSpec((1, D), lambda b, s: (0, 0))
    return pl.pallas_call(
        _out_kernel,
        grid=(B, S // tm),
        in_specs=[att_spec, att_spec,
                  pl.BlockSpec((1, tm, 2 * D_BRANCH), lambda b, s: (b, s, 0)),
                  row_spec,
                  pl.BlockSpec((1, 3, D), lambda b, s: (b, 0, 0)),
                  pl.BlockSpec(w_out_bf.shape, lambda b, s: (0, 0)),
                  vec_spec, vec_spec],
        out_specs=row_spec,
        out_shape=jax.ShapeDtypeStruct((B, S, D), f32),
        scratch_shapes=[pltpu.VMEM((tm, 2 * D_BRANCH), bf16)],
        compiler_params=pltpu.CompilerParams(
            dimension_semantics=("arbitrary", "arbitrary"), vmem_limit_bytes=VMEM_LIMIT),
        name="out_proj_ln",
    )(o_a, o_b, g, x, mod3, w_out_bf, ln_g.reshape(1, D), ln_b.reshape(1, D))


def _rope_tables(S):
    half = HEAD_DIM // 2
    inv = ROPE_THETA ** (-jnp.arange(half, dtype=f32) / half)
    ang = jnp.arange(S, dtype=jnp.int32).astype(f32)[:, None] * inv[None, :]
    cos, sin = jnp.cos(ang), jnp.sin(ang)
    cos_head = jnp.concatenate([cos, cos], axis=-1)
    sin_head = jnp.concatenate([-sin, sin], axis=-1)
    return (jnp.tile(cos_head, (1, HEADS_PER_GROUP)), jnp.tile(sin_head, (1, HEADS_PER_GROUP)))


def kernel(x, c, w_in, w_out, w_ada, b_ada, ln_g, ln_b):
    B, S, D = x.shape
    cos_t, sin_t = _rope_tables(S)
    for layer in range(w_in.shape[0]):
        mod3 = _modulation(c, w_ada[layer], b_ada[layer]).reshape(B, 3, D)
        qa, ka, va, qb, kb, vb, g = _projection(x, mod3, cos_t, sin_t, w_in[layer].astype(bf16))
        o_a = _dilated_attention(qa, ka, va)
        o_b = _moba_attention(qb, kb, vb)
        x = _out_projection(o_a, o_b, g, x, mod3, w_out[layer].astype(bf16), ln_g[layer], ln_b[layer])
    return x
```

```python
import numpy as np
import jax
import jax.numpy as jnp
from jax import lax
from jax.experimental import pallas as pl
from jax.experimental.pallas import tpu as pltpu

HEAD_DIM = 64
N_HEADS = 8
D_BRANCH = N_HEADS * HEAD_DIM
DILATED_PATTERNS = ((128, 1), (512, 4), (2048, 16))
MOBA_BLOCK = 256
MOBA_TOPK = 3
ROPE_THETA = 10000.0
LN_EPS = 1e-5
NEG_INF = -1e30
DEPTH = 1
DEEPNORM_ALPHA = (2.0 * DEPTH) ** 0.25
LOG2E = 1.4426950408889634
Q_SCALE = HEAD_DIM ** -0.5 * LOG2E

LANES = 128
HEADS_PER_GROUP = LANES // HEAD_DIM
N_GROUPS = N_HEADS // HEADS_PER_GROUP
ATT_TILE = 256
PROJ_ROWS = 512
VMEM_LIMIT = 48 * 1024 * 1024

NT_DIMS = (((1,), (1,)), ((), ()))

f32 = jnp.float32
bf16 = jnp.bfloat16


def _mod_kernel(c_ref, w_ref, b_ref, o_ref):
    o_ref[...] = jnp.dot(c_ref[...], w_ref[...], preferred_element_type=f32,
                         precision=lax.Precision.HIGHEST) + b_ref[...]


def _modulation(c, w_ada, b_ada):
    B, D = c.shape
    N = w_ada.shape[1]
    bn = 512
    return pl.pallas_call(
        _mod_kernel,
        grid=(N // bn,),
        in_specs=[pl.BlockSpec((B, D), lambda n: (0, 0)),
                  pl.BlockSpec((D, bn), lambda n: (0, n)),
                  pl.BlockSpec((1, bn), lambda n: (0, n))],
        out_specs=pl.BlockSpec((B, bn), lambda n: (0, n)),
        out_shape=jax.ShapeDtypeStruct((B, N), f32),
        name="adaln_mod",
    )(c, w_ada, b_ada.reshape(1, N))


def _proj_kernel(x_ref, mod_ref, cos_ref, sin_ref, w_ref,
                 qa_ref, ka_ref, va_ref, qb_ref, kb_ref, vb_ref, g_ref):
    tm = x_ref.shape[1]
    shift = mod_ref[0, 0:1, :]
    scale = mod_ref[0, 1:2, :]
    h = (x_ref[0] * (1.0 + scale) + shift).astype(bf16)
    cos = cos_ref[...]
    sin = sin_ref[...]
    lane = lax.broadcasted_iota(jnp.int32, (tm, LANES), 1)
    first_half = (lane & (HEAD_DIM // 2)) == 0

    def rope(t):
        partner = jnp.where(first_half,
                            pltpu.roll(t, LANES - HEAD_DIM // 2, 1),
                            pltpu.roll(t, HEAD_DIM // 2, 1))
        return t * cos + partner * sin

    targets = (qa_ref, ka_ref, va_ref, None, qb_ref, kb_ref, vb_ref, None)
    kinds = ("q", "k", "v", "g", "q", "k", "v", "g")
    for c in range(8):
        acc = jnp.dot(h, w_ref[:, c * D_BRANCH:(c + 1) * D_BRANCH],
                      preferred_element_type=f32)
        if kinds[c] == "g":
            off = 0 if c == 3 else D_BRANCH
            g_ref[0, :, off:off + D_BRANCH] = acc.astype(bf16)
            continue
        for gi in range(N_GROUPS):
            t = acc[:, gi * LANES:(gi + 1) * LANES]
            if kinds[c] == "q":
                t = rope(t) * Q_SCALE
            elif kinds[c] == "k":
                t = rope(t)
            targets[c][0, gi] = t.astype(bf16)


def _projection(x, mod3, cos_t, sin_t, w_in_bf):
    B, S, D = x.shape
    tm = PROJ_ROWS
    n_s = S // tm
    qkv_shape = jax.ShapeDtypeStruct((B, N_GROUPS, S, LANES), bf16)
    qkv_spec = pl.BlockSpec((1, N_GROUPS, tm, LANES), lambda b, s: (b, 0, s, 0))
    return pl.pallas_call(
        _proj_kernel,
        grid=(B, n_s),
        in_specs=[pl.BlockSpec((1, tm, D), lambda b, s: (b, s, 0)),
                  pl.BlockSpec((1, 3, D), lambda b, s: (b, 0, 0)),
                  pl.BlockSpec((tm, LANES), lambda b, s: (s, 0)),
                  pl.BlockSpec((tm, LANES), lambda b, s: (s, 0)),
                  pl.BlockSpec(w_in_bf.shape, lambda b, s: (0, 0))],
        out_specs=[qkv_spec] * 6 + [pl.BlockSpec((1, tm, 2 * D_BRANCH), lambda b, s: (b, s, 0))],
        out_shape=[qkv_shape] * 6 + [jax.ShapeDtypeStruct((B, S, 2 * D_BRANCH), bf16)],
        compiler_params=pltpu.CompilerParams(
            dimension_semantics=("arbitrary", "arbitrary"), vmem_limit_bytes=VMEM_LIMIT),
        name="in_proj_rope",
    )(x, mod3, cos_t, sin_t, w_in_bf)


def _own_lanes(h, shape):
    lane = lax.broadcasted_iota(jnp.int32, shape, len(shape) - 1)
    return (lane < HEAD_DIM) if h == 0 else (lane >= HEAD_DIM)


def _own_rows(h, shape):
    row = lax.broadcasted_iota(jnp.int32, shape, 0)
    return (row < HEAD_DIM) if h == 0 else (row >= HEAD_DIM)


def _fill_v_aug_t(v_ref, vaugt_ref):
    vt = v_ref[0, 0].astype(f32).T
    for h in range(HEADS_PER_GROUP):
        vaugt_ref[h] = jnp.where(_own_rows(h, vt.shape), vt, 1.0).astype(bf16)


class _Rows:
    def __init__(self, i, h, key_bias, query_bias):
        self.i, self.h = i, h
        self.key_bias, self.query_bias = key_bias, query_bias
        self.qh = self.mrun = self.m = self.acc = None


def _score_pass(u, q_ref, k_ref, s_ref, slot):
    T = ATT_TILE
    n = (u.i + 1) * T
    q = q_ref[0, 0, u.i * T:(u.i + 1) * T, :]
    qh = jnp.where(_own_lanes(u.h, q.shape), q, jnp.zeros_like(q))
    s = lax.dot_general(k_ref[0, 0, 0:n, :], qh, NT_DIMS, preferred_element_type=f32)
    kbs = [u.key_bias(j) for j in range(u.i + 1)]
    if all(kb is not None for kb in kbs):
        s = s + jnp.concatenate(kbs, axis=0) if len(kbs) > 1 else s + kbs[0]
    else:
        s = jnp.concatenate([s[j * T:(j + 1) * T] if kb is None else s[j * T:(j + 1) * T] + kb
                             for j, kb in enumerate(kbs)], axis=0) if len(kbs) > 1 else \
            (s if kbs[0] is None else s + kbs[0])
    s_ref[slot, 0:n, :] = s
    mrun = None
    for j in range(u.i + 1):
        t = jnp.max(s[j * T:(j + 1) * T].reshape(T // 8, 8, T), axis=0)
        qb = u.query_bias(j)
        if qb is not None:
            t = t + qb
        mrun = t if mrun is None else jnp.maximum(mrun, t)
    u.m = jnp.max(mrun, axis=0, keepdims=True)


def _value_pass(u, s_ref, slot, vaugt_ref):
    T = ATT_TILE
    n = (u.i + 1) * T
    ps = []
    for j in range(u.i + 1):
        m = u.m
        qb = u.query_bias(j)
        if qb is not None:
            m = m - qb
        ps.append(jnp.exp2(s_ref[slot, j * T:(j + 1) * T, :] - m).astype(bf16))
    p = jnp.concatenate(ps, axis=0) if len(ps) > 1 else ps[0]
    u.acc = jnp.dot(vaugt_ref[u.h, :, 0:n], p, preferred_element_type=f32)


def _run_rows(units, q_ref, k_ref, s_ref, vaugt_ref, o_ref):
    T = ATT_TILE
    done = {}
    prev = None
    for idx, u in enumerate(list(units) + [None]):
        if u is not None:
            _score_pass(u, q_ref, k_ref, s_ref, idx % 2)
        if prev is not None:
            _value_pass(prev, s_ref, (idx - 1) % 2, vaugt_ref)
        if prev is not None:
            den_row = HEAD_DIM if prev.h == 0 else 0
            done[prev.h] = prev.acc / prev.acc[den_row:den_row + 1, :]
            if len(done) == HEADS_PER_GROUP:
                out_t = jnp.where(_own_rows(0, done[0].shape), done[0], done[1])
                o_ref[0, 0, prev.i * T:(prev.i + 1) * T, :] = out_t.T.astype(o_ref.dtype)
                done = {}
        prev = u


def _dilated_kernel(q_ref, k_ref, v_ref, slab_ref, o_ref, s_ref, vaugt_ref):
    T = ATT_TILE
    n_t = q_ref.shape[2] // T
    _fill_v_aug_t(v_ref, vaugt_ref)
    units = [_Rows(i, h,
                   lambda j, i=i: slab_ref[(n_t - 1 - i + j) * T:(n_t - i + j) * T, :],
                   lambda j: None)
             for i in range(n_t) for h in range(HEADS_PER_GROUP)]
    _run_rows(units, q_ref, k_ref, s_ref, vaugt_ref, o_ref)


def _moba_kernel(q_ref, k_ref, v_ref, o_ref, s_ref, vaugt_ref, kmean_ref):
    T = ATT_TILE
    S = q_ref.shape[2]
    n_t = S // T
    _fill_v_aug_t(v_ref, vaugt_ref)

    kmean_ref[...] = jnp.zeros_like(kmean_ref)
    for j in range(n_t):
        kj = k_ref[0, 0, j * T:(j + 1) * T, :]
        kmean_ref[j:j + 1, :] = jnp.sum(kj.astype(f32), axis=0, keepdims=True) * (1.0 / T)
    kmean = kmean_ref[...]
    km_hi = kmean.astype(bf16)
    km_lo = (kmean - km_hi.astype(f32)).astype(bf16)

    key_i = lax.broadcasted_iota(jnp.int32, (T, T), 0)
    qry_i = lax.broadcasted_iota(jnp.int32, (T, T), 1)
    causal_bias = jnp.where(key_i <= qry_i, 0.0, NEG_INF).astype(f32)

    blk = lax.broadcasted_iota(jnp.int32, (8, S), 0)
    own_blk = lax.broadcasted_iota(jnp.int32, (8, S), 1) // T
    past = blk < own_blk
    q_all = q_ref[0, 0]
    drop = []
    for h in range(HEADS_PER_GROUP):
        qh_all = jnp.where(_own_lanes(h, q_all.shape), q_all, jnp.zeros_like(q_all))
        gate = (lax.dot_general(km_hi, qh_all, NT_DIMS, preferred_element_type=f32)
                + lax.dot_general(km_lo, qh_all, NT_DIMS, preferred_element_type=f32))[0:8]
        cnt = jnp.zeros((8, S), f32)
        for jp in range(n_t):
            gj = gate[jp:jp + 1, :]
            ahead = (gj > gate) | ((gj == gate) & (jp < blk))
            cnt = cnt + jnp.where(ahead & (jp < own_blk), 1.0, 0.0)
        drop.append(jnp.where(past & (cnt >= float(MOBA_TOPK)), NEG_INF, 0.0))

    units = [_Rows(i, h,
                   lambda j, i=i: causal_bias if j == i else None,
                   lambda j, i=i, h=h: None if j == i else drop[h][j:j + 1, i * T:(i + 1) * T])
             for i in range(n_t) for h in range(HEADS_PER_GROUP)]
    _run_rows(units, q_ref, k_ref, s_ref, vaugt_ref, o_ref)


def _dilated_bias_slab(n_t):
    T = ATT_TILE
    key = np.arange(T)[:, None]
    qry = np.arange(T)[None, :]
    slab = np.empty((n_t * T, T), np.float32)
    for b in range(n_t):
        dist = (n_t - 1 - b) * T + qry - key
        mult = np.zeros((T, T), np.float64)
        for window, dil in DILATED_PATTERNS:
            mult += (dist >= 0) & (dist <= window) & (dist % dil == 0)
        slab[b * T:(b + 1) * T, :] = np.where(mult > 0, np.log2(np.maximum(mult, 1.0)), NEG_INF)
    return jnp.asarray(slab)


def _attention_call(body, name, q, k, v, consts, extra_scratch):
    B, G, S, _ = q.shape
    blk = pl.BlockSpec((1, 1, S, LANES), lambda b, g: (b, g, 0, 0))
    scratch = [pltpu.VMEM((2, S, ATT_TILE), f32),
               pltpu.VMEM((HEADS_PER_GROUP, LANES, S), bf16)]
    return pl.pallas_call(
        body,
        grid=(B, G),
        in_specs=[blk, blk, blk] + [pl.BlockSpec(c.shape, lambda b, g: (0, 0)) for c in consts],
        out_specs=blk,
        out_shape=jax.ShapeDtypeStruct(q.shape, bf16),
        scratch_shapes=scratch + extra_scratch,
        compiler_params=pltpu.CompilerParams(
            dimension_semantics=("arbitrary", "arbitrary"), vmem_limit_bytes=VMEM_LIMIT),
        name=name,
    )(q, k, v, *consts)


def _dilated_attention(q, k, v):
    S = q.shape[2]
    return _attention_call(_dilated_kernel, "dilated_attn", q, k, v,
                           [_dilated_bias_slab(S // ATT_TILE)], [])


def _moba_attention(q, k, v):
    S = q.shape[2]
    assert ATT_TILE == MOBA_BLOCK and S // ATT_TILE <= 8
    return _attention_call(_moba_kernel, "moba_attn", q, k, v, [],
                           [pltpu.VMEM((16, LANES), f32)])


def _out_kernel(oa_ref, ob_ref, g_ref, x_ref, mod_ref, w_ref, lng_ref, lnb_ref, o_ref, og_ref):
    for gi in range(N_GROUPS):
        for br, src in enumerate((oa_ref, ob_ref)):
            lo = br * D_BRANCH + gi * LANES
            g = g_ref[0, :, lo:lo + LANES].astype(f32)
            silu = g / (1.0 + jnp.exp(-g))
            og_ref[:, lo:lo + LANES] = (src[0, gi].astype(f32) * silu).astype(bf16)
    y = jnp.dot(og_ref[...], w_ref[...], preferred_element_type=f32)
    gate = mod_ref[0, 2:3, :]
    z = DEEPNORM_ALPHA * x_ref[0] + gate * y
    mu = jnp.mean(z, axis=-1, keepdims=True)
    zc = z - mu
    var = jnp.mean(zc * zc, axis=-1, keepdims=True)
    o_ref[0] = zc * lax.rsqrt(var + LN_EPS) * lng_ref[...] + lnb_ref[...]


def _out_projection(o_a, o_b, g, x, mod3, w_out_bf, ln_g, ln_b):
    B, S, D = x.shape
    tm = PROJ_ROWS
    att_spec = pl.BlockSpec((1, N_GROUPS, tm, LANES), lambda b, s: (b, 0, s, 0))
    row_spec = pl.BlockSpec((1, tm, D), lambda b, s: (b, s, 0))
    vec_spec = pl.BlockSpec((1, D), lambda b, s: (0, 0))
    return pl.pallas_call(
        _out_kernel,
        grid=(B, S // tm),
        in_specs=[att_spec, att_spec,
                  pl.BlockSpec((1, tm, 2 * D_BRANCH), lambda b, s: (b, s, 0)),
                  row_spec,
                  pl.BlockSpec((1, 3, D), lambda b, s: (b, 0, 0)),
                  pl.BlockSpec(w_out_bf.shape, lambda b, s: (0, 0)),
                  vec_spec, vec_spec],
        out_specs=row_spec,
        out_shape=jax.ShapeDtypeStruct((B, S, D), f32),
        scratch_shapes=[pltpu.VMEM((tm, 2 * D_BRANCH), bf16)],
        compiler_params=pltpu.CompilerParams(
            dimension_semantics=("arbitrary", "arbitrary"), vmem_limit_bytes=VMEM_LIMIT),
        name="out_proj_ln",
    )(o_a, o_b, g, x, mod3, w_out_bf, ln_g.reshape(1, D), ln_b.reshape(1, D))


def _rope_tables(S):
    half = HEAD_DIM // 2
    inv = ROPE_THETA ** (-jnp.arange(half, dtype=f32) / half)
    ang = jnp.arange(S, dtype=jnp.int32).astype(f32)[:, None] * inv[None, :]
    cos, sin = jnp.cos(ang), jnp.sin(ang)
    cos_head = jnp.concatenate([cos, cos], axis=-1)
    sin_head = jnp.concatenate([-sin, sin], axis=-1)
    return (jnp.tile(cos_head, (1, HEADS_PER_GROUP)), jnp.tile(sin_head, (1, HEADS_PER_GROUP)))


def kernel(x, c, w_in, w_out, w_ada, b_ada, ln_g, ln_b):
    B, S, D = x.shape
    cos_t, sin_t = _rope_tables(S)
    for layer in range(w_in.shape[0]):
        mod3 = _modulation(c, w_ada[layer], b_ada[layer]).reshape(B, 3, D)
        qa, ka, va, qb, kb, vb, g = _projection(x, mod3, cos_t, sin_t, w_in[layer].astype(bf16))
        o_a = _dilated_attention(qa, ka, va)
        o_b = _moba_attention(qb, kb, vb)
        x = _out_projection(o_a, o_b, g, x, mod3, w_out[layer].astype(bf16), ln_g[layer], ln_b[layer])
    return x
```

```python
import numpy as np
import jax
import jax.numpy as jnp
from jax import lax
from jax.experimental import pallas as pl
from jax.experimental.pallas import tpu as pltpu

HEAD_DIM = 64
N_HEADS = 8
D_BRANCH = N_HEADS * HEAD_DIM
DILATED_PATTERNS = ((128, 1), (512, 4), (2048, 16))
MOBA_BLOCK = 256
MOBA_TOPK = 3
ROPE_THETA = 10000.0
LN_EPS = 1e-5
NEG_INF = -1e30
DEPTH = 1
DEEPNORM_ALPHA = (2.0 * DEPTH) ** 0.25
LOG2E = 1.4426950408889634
Q_SCALE = HEAD_DIM ** -0.5 * LOG2E

LANES = 128
HEADS_PER_GROUP = LANES // HEAD_DIM
N_GROUPS = N_HEADS // HEADS_PER_GROUP
ATT_TILE = 256
N_SLOTS = 3
SCORE_CHUNK = 2
STAGE_ORDER = (("S", 0), ("E", 2), ("V", 4))
PROJ_ROWS = 512
VMEM_LIMIT = 48 * 1024 * 1024

NT_DIMS = (((1,), (1,)), ((), ()))

f32 = jnp.float32
bf16 = jnp.bfloat16


def _mod_kernel(c_ref, w_ref, b_ref, o_ref):
    o_ref[...] = jnp.dot(c_ref[...], w_ref[...], preferred_element_type=f32,
                         precision=lax.Precision.HIGHEST) + b_ref[...]


def _modulation(c, w_ada, b_ada):
    B, D = c.shape
    N = w_ada.shape[1]
    bn = 512
    return pl.pallas_call(
        _mod_kernel,
        grid=(N // bn,),
        in_specs=[pl.BlockSpec((B, D), lambda n: (0, 0)),
                  pl.BlockSpec((D, bn), lambda n: (0, n)),
                  pl.BlockSpec((1, bn), lambda n: (0, n))],
        out_specs=pl.BlockSpec((B, bn), lambda n: (0, n)),
        out_shape=jax.ShapeDtypeStruct((B, N), f32),
        name="adaln_mod",
    )(c, w_ada, b_ada.reshape(1, N))


def _proj_kernel(x_ref, mod_ref, cos_ref, sin_ref, w_ref,
                 qa_ref, ka_ref, va_ref, qb_ref, kb_ref, vb_ref, g_ref):
    tm = x_ref.shape[1]
    shift = mod_ref[0, 0:1, :]
    scale = mod_ref[0, 1:2, :]
    h = (x_ref[0] * (1.0 + scale) + shift).astype(bf16)
    cos = cos_ref[...]
    sin = sin_ref[...]
    lane = lax.broadcasted_iota(jnp.int32, (tm, LANES), 1)
    first_half = (lane & (HEAD_DIM // 2)) == 0

    def rope(t):
        partner = jnp.where(first_half,
                            pltpu.roll(t, LANES - HEAD_DIM // 2, 1),
                            pltpu.roll(t, HEAD_DIM // 2, 1))
        return t * cos + partner * sin

    targets = (qa_ref, ka_ref, va_ref, None, qb_ref, kb_ref, vb_ref, None)
    kinds = ("q", "k", "v", "g", "q", "k", "v", "g")
    for c in range(8):
        acc = jnp.dot(h, w_ref[:, c * D_BRANCH:(c + 1) * D_BRANCH],
                      preferred_element_type=f32)
        if kinds[c] == "g":
            off = 0 if c == 3 else D_BRANCH
            g_ref[0, :, off:off + D_BRANCH] = acc.astype(bf16)
            continue
        for gi in range(N_GROUPS):
            t = acc[:, gi * LANES:(gi + 1) * LANES]
            if kinds[c] == "q":
                t = rope(t) * Q_SCALE
            elif kinds[c] == "k":
                t = rope(t)
            targets[c][0, gi] = t.astype(bf16)


def _projection(x, mod3, cos_t, sin_t, w_in_bf):
    B, S, D = x.shape
    tm = PROJ_ROWS
    n_s = S // tm
    qkv_shape = jax.ShapeDtypeStruct((B, N_GROUPS, S, LANES), bf16)
    qkv_spec = pl.BlockSpec((1, N_GROUPS, tm, LANES), lambda b, s: (b, 0, s, 0))
    return pl.pallas_call(
        _proj_kernel,
        grid=(B, n_s),
        in_specs=[pl.BlockSpec((1, tm, D), lambda b, s: (b, s, 0)),
                  pl.BlockSpec((1, 3, D), lambda b, s: (b, 0, 0)),
                  pl.BlockSpec((tm, LANES), lambda b, s: (s, 0)),
                  pl.BlockSpec((tm, LANES), lambda b, s: (s, 0)),
                  pl.BlockSpec(w_in_bf.shape, lambda b, s: (0, 0))],
        out_specs=[qkv_spec] * 6 + [pl.BlockSpec((1, tm, 2 * D_BRANCH), lambda b, s: (b, s, 0))],
        out_shape=[qkv_shape] * 6 + [jax.ShapeDtypeStruct((B, S, 2 * D_BRANCH), bf16)],
        compiler_params=pltpu.CompilerParams(
            dimension_semantics=("arbitrary", "arbitrary"), vmem_limit_bytes=VMEM_LIMIT),
        name="in_proj_rope",
    )(x, mod3, cos_t, sin_t, w_in_bf)


def _own_lanes(h, shape):
    lane = lax.broadcasted_iota(jnp.int32, shape, len(shape) - 1)
    return (lane < HEAD_DIM) if h == 0 else (lane >= HEAD_DIM)


def _own_rows(h, shape):
    row = lax.broadcasted_iota(jnp.int32, shape, 0)
    return (row < HEAD_DIM) if h == 0 else (row >= HEAD_DIM)


def _fill_v_aug_t(v_ref, vaugt_ref):
    vt = v_ref[0, 0].astype(f32).T
    for h in range(HEADS_PER_GROUP):
        vaugt_ref[h] = jnp.where(_own_rows(h, vt.shape), vt, 1.0).astype(bf16)


class _Rows:
    def __init__(self, i, h, key_bias, query_bias):
        self.i, self.h = i, h
        self.key_bias, self.query_bias = key_bias, query_bias
        self.qh = self.mrun = self.m = self.acc = None


def _score_pass(u, q_ref, k_ref, s_ref, slot):
    T = ATT_TILE
    q = q_ref[0, 0, u.i * T:(u.i + 1) * T, :]
    qh = jnp.where(_own_lanes(u.h, q.shape), q, jnp.zeros_like(q))
    mrun = None
    for j0 in range(0, u.i + 1, SCORE_CHUNK):
        js = range(j0, min(j0 + SCORE_CHUNK, u.i + 1))
        s = lax.dot_general(k_ref[0, 0, js[0] * T:(js[-1] + 1) * T, :], qh, NT_DIMS,
                            preferred_element_type=f32)
        for j in js:
            sj = s[(j - j0) * T:(j - j0 + 1) * T]
            kb = u.key_bias(j)
            if kb is not None:
                sj = sj + kb
            s_ref[slot, j * T:(j + 1) * T, :] = sj
            t = jnp.max(sj.reshape(T // 8, 8, T), axis=0)
            qb = u.query_bias(j)
            if qb is not None:
                t = t + qb
            mrun = t if mrun is None else jnp.maximum(mrun, t)
    u.m = jnp.max(mrun, axis=0, keepdims=True)


def _exp_pass(u, s_ref, p_ref, slot):
    T = ATT_TILE
    for j in range(u.i + 1):
        m = u.m
        qb = u.query_bias(j)
        if qb is not None:
            m = m - qb
        rows = slice(j * T, (j + 1) * T)
        p_ref[slot, rows, :] = jnp.exp2(s_ref[slot, rows, :] - m).astype(bf16)


def _value_pass(u, p_ref, slot, vaugt_ref):
    n = (u.i + 1) * ATT_TILE
    u.acc = jnp.dot(vaugt_ref[u.h, :, 0:n], p_ref[slot, 0:n, :],
                    preferred_element_type=f32)


def _run_rows(units, q_ref, k_ref, s_ref, p_ref, vaugt_ref, o_ref):
    T = ATT_TILE
    done = {}
    units = list(units)

    def score(idx):
        if 0 <= idx < len(units):
            _score_pass(units[idx], q_ref, k_ref, s_ref, idx % N_SLOTS)

    def exp(idx):
        if 0 <= idx < len(units):
            _exp_pass(units[idx], s_ref, p_ref, idx % N_SLOTS)

    def value(idx):
        if 0 <= idx < len(units):
            prev = units[idx]
            _value_pass(prev, p_ref, idx % N_SLOTS, vaugt_ref)
            den_row = HEAD_DIM if prev.h == 0 else 0
            done[prev.h] = prev.acc / prev.acc[den_row:den_row + 1, :]
            if len(done) == HEADS_PER_GROUP:
                out_t = jnp.where(_own_rows(0, done[0].shape), done[0], done[1])
                o_ref[0, 0, prev.i * T:(prev.i + 1) * T, :] = out_t.T.astype(o_ref.dtype)
                done.clear()

    stages = {"S": score, "E": exp, "V": value}
    for idx in range(len(units) + max(lag for _, lag in STAGE_ORDER)):
        for name, lag in STAGE_ORDER:
            stages[name](idx - lag)


def _dilated_kernel(q_ref, k_ref, v_ref, slab_ref, o_ref, s_ref, p_ref, vaugt_ref):
    T = ATT_TILE
    n_t = q_ref.shape[2] // T
    _fill_v_aug_t(v_ref, vaugt_ref)
    units = [_Rows(i, h,
                   lambda j, i=i: slab_ref[(n_t - 1 - i + j) * T:(n_t - i + j) * T, :],
                   lambda j: None)
             for i in range(n_t) for h in range(HEADS_PER_GROUP)]
    _run_rows(units, q_ref, k_ref, s_ref, p_ref, vaugt_ref, o_ref)


def _moba_kernel(q_ref, k_ref, v_ref, o_ref, s_ref, p_ref, vaugt_ref, kmean_ref):
    T = ATT_TILE
    S = q_ref.shape[2]
    n_t = S // T
    _fill_v_aug_t(v_ref, vaugt_ref)

    kmean_ref[...] = jnp.zeros_like(kmean_ref)
    for j in range(n_t):
        kj = k_ref[0, 0, j * T:(j + 1) * T, :]
        kmean_ref[j:j + 1, :] = jnp.sum(kj.astype(f32), axis=0, keepdims=True) * (1.0 / T)
    kmean = kmean_ref[...]
    km_hi = kmean.astype(bf16)
    km_lo = (kmean - km_hi.astype(f32)).astype(bf16)

    key_i = lax.broadcasted_iota(jnp.int32, (T, T), 0)
    qry_i = lax.broadcasted_iota(jnp.int32, (T, T), 1)
    causal_bias = jnp.where(key_i <= qry_i, 0.0, NEG_INF).astype(f32)

    blk = lax.broadcasted_iota(jnp.int32, (8, S), 0)
    own_blk = lax.broadcasted_iota(jnp.int32, (8, S), 1) // T
    past = blk < own_blk
    q_all = q_ref[0, 0]
    drop = []
    for h in range(HEADS_PER_GROUP):
        qh_all = jnp.where(_own_lanes(h, q_all.shape), q_all, jnp.zeros_like(q_all))
        gate = (lax.dot_general(km_hi, qh_all, NT_DIMS, preferred_element_type=f32)
                + lax.dot_general(km_lo, qh_all, NT_DIMS, preferred_element_type=f32))[0:8]
        cnt = jnp.zeros((8, S), f32)
        for jp in range(n_t):
            gj = gate[jp:jp + 1, :]
            ahead = (gj > gate) | ((gj == gate) & (jp < blk))
            cnt = cnt + jnp.where(ahead & (jp < own_blk), 1.0, 0.0)
        drop.append(jnp.where(past & (cnt >= float(MOBA_TOPK)), NEG_INF, 0.0))

    units = [_Rows(i, h,
                   lambda j, i=i: causal_bias if j == i else None,
                   lambda j, i=i, h=h: None if j == i else drop[h][j:j + 1, i * T:(i + 1) * T])
             for i in range(n_t) for h in range(HEADS_PER_GROUP)]
    _run_rows(units, q_ref, k_ref, s_ref, p_ref, vaugt_ref, o_ref)


def _dilated_bias_slab(n_t):
    T = ATT_TILE
    key = np.arange(T)[:, None]
    qry = np.arange(T)[None, :]
    slab = np.empty((n_t * T, T), np.float32)
    for b in range(n_t):
        dist = (n_t - 1 - b) * T + qry - key
        mult = np.zeros((T, T), np.float64)
        for window, dil in DILATED_PATTERNS:
            mult += (dist >= 0) & (dist <= window) & (dist % dil == 0)
        slab[b * T:(b + 1) * T, :] = np.where(mult > 0, np.log2(np.maximum(mult, 1.0)), NEG_INF)
    return jnp.asarray(slab)


def _attention_call(body, name, q, k, v, consts, extra_scratch):
    B, G, S, _ = q.shape
    blk = pl.BlockSpec((1, 1, S, LANES), lambda b, g: (b, g, 0, 0))
    scratch = [pltpu.VMEM((N_SLOTS, S, ATT_TILE), f32),
               pltpu.VMEM((N_SLOTS, S, ATT_TILE), bf16),
               pltpu.VMEM((HEADS_PER_GROUP, LANES, S), bf16)]
    return pl.pallas_call(
        body,
        grid=(B, G),
        in_specs=[blk, blk, blk] + [pl.BlockSpec(c.shape, lambda b, g: (0, 0)) for c in consts],
        out_specs=blk,
        out_shape=jax.ShapeDtypeStruct(q.shape, bf16),
        scratch_shapes=scratch + extra_scratch,
        compiler_params=pltpu.CompilerParams(
            dimension_semantics=("arbitrary", "arbitrary"), vmem_limit_bytes=VMEM_LIMIT),
        name=name,
    )(q, k, v, *consts)


def _dilated_attention(q, k, v):
    S = q.shape[2]
    return _attention_call(_dilated_kernel, "dilated_attn", q, k, v,
                           [_dilated_bias_slab(S // ATT_TILE)], [])


def _moba_attention(q, k, v):
    S = q.shape[2]
    assert ATT_TILE == MOBA_BLOCK and S // ATT_TILE <= 8
    return _attention_call(_moba_kernel, "moba_attn", q, k, v, [],
                           [pltpu.VMEM((16, LANES), f32)])


def _out_kernel(oa_ref, ob_ref, g_ref, x_ref, mod_ref, w_ref, lng_ref, lnb_ref, o_ref, og_ref):
    for gi in range(N_GROUPS):
        for br, src in enumerate((oa_ref, ob_ref)):
            lo = br * D_BRANCH + gi * LANES
            g = g_ref[0, :, lo:lo + LANES].astype(f32)
            silu = g / (1.0 + jnp.exp(-g))
            og_ref[:, lo:lo + LANES] = (src[0, gi].astype(f32) * silu).astype(bf16)
    y = jnp.dot(og_ref[...], w_ref[...], preferred_element_type=f32)
    gate = mod_ref[0, 2:3, :]
    z = DEEPNORM_ALPHA * x_ref[0] + gate * y
    mu = jnp.mean(z, axis=-1, keepdims=True)
    zc = z - mu
    var = jnp.mean(zc * zc, axis=-1, keepdims=True)
    o_ref[0] = zc * lax.rsqrt(var + LN_EPS) * lng_ref[...] + lnb_ref[...]


def _out_projection(o_a, o_b, g, x, mod3, w_out_bf, ln_g, ln_b):
    B, S, D = x.shape
    tm = PROJ_ROWS
    att_spec = pl.BlockSpec((1, N_GROUPS, tm, LANES), lambda b, s: (b, 0, s, 0))
    row_spec = pl.BlockSpec((1, tm, D), lambda b, s: (b, s, 0))
    vec_spec = pl.BlockSpec((1, D), lambda b, s: (0, 0))
    return pl.pallas_call(
        _out_kernel,
        grid=(B, S // tm),
        in_specs=[att_spec, att_spec,
                  pl.BlockSpec((1, tm, 2 * D_BRANCH), lambda b, s: (b, s, 0)),
                  row_spec,
                  pl.BlockSpec((1, 3, D), lambda b, s: (b, 0, 0)),
                  pl.BlockSpec(w_out_bf.shape, lambda b, s: (0, 0)),
                  vec_spec, vec_spec],
        out_specs=row_spec,
        out_shape=jax.ShapeDtypeStruct((B, S, D), f32),
        scratch_shapes=[pltpu.VMEM((tm, 2 * D_BRANCH), bf16)],
        compiler_params=pltpu.CompilerParams(
            dimension_semantics=("arbitrary", "arbitrary"), vmem_limit_bytes=VMEM_LIMIT),
        name="out_proj_ln",
    )(o_a, o_b, g, x, mod3, w_out_bf, ln_g.reshape(1, D), ln_b.reshape(1, D))


def _rope_tables(S):
    half = HEAD_DIM // 2
    inv = ROPE_THETA ** (-jnp.arange(half, dtype=f32) / half)
    ang = jnp.arange(S, dtype=jnp.int32).astype(f32)[:, None] * inv[None, :]
    cos, sin = jnp.cos(ang), jnp.sin(ang)
    cos_head = jnp.concatenate([cos, cos], axis=-1)
    sin_head = jnp.concatenate([-sin, sin], axis=-1)
    return (jnp.tile(cos_head, (1, HEADS_PER_GROUP)), jnp.tile(sin_head, (1, HEADS_PER_GROUP)))


def kernel(x, c, w_in, w_out, w_ada, b_ada, ln_g, ln_b):
    B, S, D = x.shape
    cos_t, sin_t = _rope_tables(S)
    for layer in range(w_in.shape[0]):
        mod3 = _modulation(c, w_ada[layer], b_ada[layer]).reshape(B, 3, D)
        qa, ka, va, qb, kb, vb, g = _projection(x, mod3, cos_t, sin_t, w_in[layer].astype(bf16))
        o_a = _dilated_attention(qa, ka, va)
        o_b = _moba_attention(qb, kb, vb)
        x = _out_projection(o_a, o_b, g, x, mod3, w_out[layer].astype(bf16), ln_g[layer], ln_b[layer])
    return x
```

```python
import numpy as np
import jax
import jax.numpy as jnp
from jax import lax
from jax.experimental import pallas as pl
from jax.experimental.pallas import tpu as pltpu

HEAD_DIM = 64
N_HEADS = 8
D_BRANCH = N_HEADS * HEAD_DIM
DILATED_PATTERNS = ((128, 1), (512, 4), (2048, 16))
MOBA_BLOCK = 256
MOBA_TOPK = 3
ROPE_THETA = 10000.0
LN_EPS = 1e-5
NEG_INF = -1e30
DEPTH = 1
DEEPNORM_ALPHA = (2.0 * DEPTH) ** 0.25
LOG2E = 1.4426950408889634
Q_SCALE = HEAD_DIM ** -0.5 * LOG2E

LANES = 128
HEADS_PER_GROUP = LANES // HEAD_DIM
N_GROUPS = N_HEADS // HEADS_PER_GROUP
ATT_TILE = 256
N_SLOTS = 3
SCORE_CHUNK = 2
STAGE_ORDER = (("S", 0), ("E", 2), ("V", 4))
PROJ_ROWS = 512
OUT_ROWS = 1024
VMEM_LIMIT = 48 * 1024 * 1024

NT_DIMS = (((1,), (1,)), ((), ()))

f32 = jnp.float32
bf16 = jnp.bfloat16


def _mod_kernel(c_ref, w_ref, b_ref, o_ref):
    o_ref[...] = jnp.dot(c_ref[...], w_ref[...], preferred_element_type=f32,
                         precision=lax.Precision.HIGHEST) + b_ref[...]


def _modulation(c, w_ada, b_ada):
    B, D = c.shape
    N = w_ada.shape[1]
    bn = 512
    return pl.pallas_call(
        _mod_kernel,
        grid=(N // bn,),
        in_specs=[pl.BlockSpec((B, D), lambda n: (0, 0)),
                  pl.BlockSpec((D, bn), lambda n: (0, n)),
                  pl.BlockSpec((1, bn), lambda n: (0, n))],
        out_specs=pl.BlockSpec((B, bn), lambda n: (0, n)),
        out_shape=jax.ShapeDtypeStruct((B, N), f32),
        name="adaln_mod",
    )(c, w_ada, b_ada.reshape(1, N))


def _proj_kernel(x_ref, mod_ref, cos_ref, sin_ref, w_ref,
                 qa_ref, ka_ref, va_ref, qb_ref, kb_ref, vb_ref, g_ref):
    tm = x_ref.shape[1]
    shift = mod_ref[0, 0:1, :]
    scale = mod_ref[0, 1:2, :]
    h = (x_ref[0] * (1.0 + scale) + shift).astype(bf16)
    cos = cos_ref[...]
    sin = sin_ref[...]
    lane = lax.broadcasted_iota(jnp.int32, (tm, LANES), 1)
    first_half = (lane & (HEAD_DIM // 2)) == 0

    def rope(t):
        partner = jnp.where(first_half,
                            pltpu.roll(t, LANES - HEAD_DIM // 2, 1),
                            pltpu.roll(t, HEAD_DIM // 2, 1))
        return t * cos + partner * sin

    targets = (qa_ref, ka_ref, va_ref, None, qb_ref, kb_ref, vb_ref, None)
    kinds = ("q", "k", "v", "g", "q", "k", "v", "g")
    for c in range(8):
        acc = jnp.dot(h, w_ref[:, c * D_BRANCH:(c + 1) * D_BRANCH],
                      preferred_element_type=f32)
        if kinds[c] == "g":
            off = 0 if c == 3 else D_BRANCH
            g_ref[0, :, off:off + D_BRANCH] = acc.astype(bf16)
            continue
        for gi in range(N_GROUPS):
            t = acc[:, gi * LANES:(gi + 1) * LANES]
            if kinds[c] == "q":
                t = rope(t) * Q_SCALE
            elif kinds[c] == "k":
                t = rope(t)
            targets[c][0, gi] = t.astype(bf16)


def _projection(x, mod3, cos_t, sin_t, w_in_bf):
    B, S, D = x.shape
    tm = PROJ_ROWS
    n_s = S // tm
    qkv_shape = jax.ShapeDtypeStruct((B, N_GROUPS, S, LANES), bf16)
    qkv_spec = pl.BlockSpec((1, N_GROUPS, tm, LANES), lambda b, s: (b, 0, s, 0))
    return pl.pallas_call(
        _proj_kernel,
        grid=(B, n_s),
        in_specs=[pl.BlockSpec((1, tm, D), lambda b, s: (b, s, 0)),
                  pl.BlockSpec((1, 3, D), lambda b, s: (b, 0, 0)),
                  pl.BlockSpec((tm, LANES), lambda b, s: (s, 0)),
                  pl.BlockSpec((tm, LANES), lambda b, s: (s, 0)),
                  pl.BlockSpec(w_in_bf.shape, lambda b, s: (0, 0))],
        out_specs=[qkv_spec] * 6 + [pl.BlockSpec((1, tm, 2 * D_BRANCH), lambda b, s: (b, s, 0))],
        out_shape=[qkv_shape] * 6 + [jax.ShapeDtypeStruct((B, S, 2 * D_BRANCH), bf16)],
        compiler_params=pltpu.CompilerParams(
            dimension_semantics=("arbitrary", "arbitrary"), vmem_limit_bytes=VMEM_LIMIT),
        name="in_proj_rope",
    )(x, mod3, cos_t, sin_t, w_in_bf)


def _own_lanes(h, shape):
    lane = lax.broadcasted_iota(jnp.int32, shape, len(shape) - 1)
    return (lane < HEAD_DIM) if h == 0 else (lane >= HEAD_DIM)


def _own_rows(h, shape):
    row = lax.broadcasted_iota(jnp.int32, shape, 0)
    return (row < HEAD_DIM) if h == 0 else (row >= HEAD_DIM)


def _fill_v_aug_t(v_ref, vaugt_ref):
    vt = v_ref[0, 0].astype(f32).T
    for h in range(HEADS_PER_GROUP):
        vaugt_ref[h] = jnp.where(_own_rows(h, vt.shape), vt, 1.0).astype(bf16)


class _Rows:
    def __init__(self, i, h, key_bias, query_bias):
        self.i, self.h = i, h
        self.key_bias, self.query_bias = key_bias, query_bias
        self.qh = self.mrun = self.m = self.acc = None


def _score_pass(u, q_ref, k_ref, s_ref, slot):
    T = ATT_TILE
    q = q_ref[0, 0, u.i * T:(u.i + 1) * T, :]
    qh = jnp.where(_own_lanes(u.h, q.shape), q, jnp.zeros_like(q))
    mrun = None
    for j0 in range(0, u.i + 1, SCORE_CHUNK):
        js = range(j0, min(j0 + SCORE_CHUNK, u.i + 1))
        s = lax.dot_general(k_ref[0, 0, js[0] * T:(js[-1] + 1) * T, :], qh, NT_DIMS,
                            preferred_element_type=f32)
        for j in js:
            sj = s[(j - j0) * T:(j - j0 + 1) * T]
            kb = u.key_bias(j)
            if kb is not None:
                sj = sj + kb
            s_ref[slot, j * T:(j + 1) * T, :] = sj
            t = jnp.max(sj.reshape(T // 8, 8, T), axis=0)
            qb = u.query_bias(j)
            if qb is not None:
                t = t + qb
            mrun = t if mrun is None else jnp.maximum(mrun, t)
    u.m = jnp.max(mrun, axis=0, keepdims=True)


def _exp_pass(u, s_ref, p_ref, slot):
    T = ATT_TILE
    for j in range(u.i + 1):
        m = u.m
        qb = u.query_bias(j)
        if qb is not None:
            m = m - qb
        rows = slice(j * T, (j + 1) * T)
        p_ref[slot, rows, :] = jnp.exp2(s_ref[slot, rows, :] - m).astype(bf16)


def _value_pass(u, p_ref, slot, vaugt_ref):
    n = (u.i + 1) * ATT_TILE
    u.acc = jnp.dot(vaugt_ref[u.h, :, 0:n], p_ref[slot, 0:n, :],
                    preferred_element_type=f32)


def _run_rows(units, q_ref, k_ref, s_ref, p_ref, vaugt_ref, o_ref):
    T = ATT_TILE
    done = {}
    units = list(units)

    def score(idx):
        if 0 <= idx < len(units):
            _score_pass(units[idx], q_ref, k_ref, s_ref, idx % N_SLOTS)

    def exp(idx):
        if 0 <= idx < len(units):
            _exp_pass(units[idx], s_ref, p_ref, idx % N_SLOTS)

    def value(idx):
        if 0 <= idx < len(units):
            prev = units[idx]
            _value_pass(prev, p_ref, idx % N_SLOTS, vaugt_ref)
            den_row = HEAD_DIM if prev.h == 0 else 0
            done[prev.h] = prev.acc / prev.acc[den_row:den_row + 1, :]
            if len(done) == HEADS_PER_GROUP:
                out_t = jnp.where(_own_rows(0, done[0].shape), done[0], done[1])
                o_ref[0, 0, prev.i * T:(prev.i + 1) * T, :] = out_t.T.astype(o_ref.dtype)
                done.clear()

    stages = {"S": score, "E": exp, "V": value}
    for idx in range(len(units) + max(lag for _, lag in STAGE_ORDER)):
        for name, lag in STAGE_ORDER:
            stages[name](idx - lag)


def _dilated_kernel(q_ref, k_ref, v_ref, slab_ref, o_ref, s_ref, p_ref, vaugt_ref):
    T = ATT_TILE
    n_t = q_ref.shape[2] // T
    _fill_v_aug_t(v_ref, vaugt_ref)
    units = [_Rows(i, h,
                   lambda j, i=i: slab_ref[(n_t - 1 - i + j) * T:(n_t - i + j) * T, :],
                   lambda j: None)
             for i in range(n_t) for h in range(HEADS_PER_GROUP)]
    _run_rows(units, q_ref, k_ref, s_ref, p_ref, vaugt_ref, o_ref)


def _moba_kernel(q_ref, k_ref, v_ref, o_ref, s_ref, p_ref, vaugt_ref, kmean_ref):
    T = ATT_TILE
    S = q_ref.shape[2]
    n_t = S // T
    _fill_v_aug_t(v_ref, vaugt_ref)

    kmean_ref[...] = jnp.zeros_like(kmean_ref)
    for j in range(n_t):
        kj = k_ref[0, 0, j * T:(j + 1) * T, :]
        kmean_ref[j:j + 1, :] = jnp.sum(kj.astype(f32), axis=0, keepdims=True) * (1.0 / T)
    kmean = kmean_ref[...]
    km_hi = kmean.astype(bf16)
    km_lo = (kmean - km_hi.astype(f32)).astype(bf16)

    key_i = lax.broadcasted_iota(jnp.int32, (T, T), 0)
    qry_i = lax.broadcasted_iota(jnp.int32, (T, T), 1)
    causal_bias = jnp.where(key_i <= qry_i, 0.0, NEG_INF).astype(f32)

    blk = lax.broadcasted_iota(jnp.int32, (8, S), 0)
    own_blk = lax.broadcasted_iota(jnp.int32, (8, S), 1) // T
    past = blk < own_blk
    q_all = q_ref[0, 0]
    drop = []
    for h in range(HEADS_PER_GROUP):
        qh_all = jnp.where(_own_lanes(h, q_all.shape), q_all, jnp.zeros_like(q_all))
        gate = (lax.dot_general(km_hi, qh_all, NT_DIMS, preferred_element_type=f32)
                + lax.dot_general(km_lo, qh_all, NT_DIMS, preferred_element_type=f32))[0:8]
        cnt = jnp.zeros((8, S), f32)
        for jp in range(n_t):
            gj = gate[jp:jp + 1, :]
            ahead = (gj > gate) | ((gj == gate) & (jp < blk))
            cnt = cnt + jnp.where(ahead & (jp < own_blk), 1.0, 0.0)
        drop.append(jnp.where(past & (cnt >= float(MOBA_TOPK)), NEG_INF, 0.0))

    units = [_Rows(i, h,
                   lambda j, i=i: causal_bias if j == i else None,
                   lambda j, i=i, h=h: None if j == i else drop[h][j:j + 1, i * T:(i + 1) * T])
             for i in range(n_t) for h in range(HEADS_PER_GROUP)]
    _run_rows(units, q_ref, k_ref, s_ref, p_ref, vaugt_ref, o_ref)


def _dilated_bias_slab(n_t):
    T = ATT_TILE
    key = np.arange(T)[:, None]
    qry = np.arange(T)[None, :]
    slab = np.empty((n_t * T, T), np.float32)
    for b in range(n_t):
        dist = (n_t - 1 - b) * T + qry - key
        mult = np.zeros((T, T), np.float64)
        for window, dil in DILATED_PATTERNS:
            mult += (dist >= 0) & (dist <= window) & (dist % dil == 0)
        slab[b * T:(b + 1) * T, :] = np.where(mult > 0, np.log2(np.maximum(mult, 1.0)), NEG_INF)
    return jnp.asarray(slab)


def _attention_call(body, name, q, k, v, consts, extra_scratch):
    B, G, S, _ = q.shape
    blk = pl.BlockSpec((1, 1, S, LANES), lambda b, g: (b, g, 0, 0))
    scratch = [pltpu.VMEM((N_SLOTS, S, ATT_TILE), f32),
               pltpu.VMEM((N_SLOTS, S, ATT_TILE), bf16),
               pltpu.VMEM((HEADS_PER_GROUP, LANES, S), bf16)]
    return pl.pallas_call(
        body,
        grid=(B, G),
        in_specs=[blk, blk, blk] + [pl.BlockSpec(c.shape, lambda b, g: (0, 0)) for c in consts],
        out_specs=blk,
        out_shape=jax.ShapeDtypeStruct(q.shape, bf16),
        scratch_shapes=scratch + extra_scratch,
        compiler_params=pltpu.CompilerParams(
            dimension_semantics=("arbitrary", "arbitrary"), vmem_limit_bytes=VMEM_LIMIT),
        name=name,
    )(q, k, v, *consts)


def _dilated_attention(q, k, v):
    S = q.shape[2]
    return _attention_call(_dilated_kernel, "dilated_attn", q, k, v,
                           [_dilated_bias_slab(S // ATT_TILE)], [])


def _moba_attention(q, k, v):
    S = q.shape[2]
    assert ATT_TILE == MOBA_BLOCK and S // ATT_TILE <= 8
    return _attention_call(_moba_kernel, "moba_attn", q, k, v, [],
                           [pltpu.VMEM((16, LANES), f32)])


def _out_kernel(oa_ref, ob_ref, g_ref, x_ref, mod_ref, w_ref, lng_ref, lnb_ref, o_ref, og_ref):
    for gi in range(N_GROUPS):
        for br, src in enumerate((oa_ref, ob_ref)):
            lo = br * D_BRANCH + gi * LANES
            half_g = 0.5 * g_ref[0, :, lo:lo + LANES].astype(f32)
            silu = half_g + half_g * jnp.tanh(half_g)
            og_ref[:, lo:lo + LANES] = (src[0, gi].astype(f32) * silu).astype(bf16)
    y = jnp.dot(og_ref[...], w_ref[...], preferred_element_type=f32)
    gate = mod_ref[0, 2:3, :]
    z = DEEPNORM_ALPHA * x_ref[0] + gate * y
    mu = jnp.mean(z, axis=-1, keepdims=True)
    zc = z - mu
    var = jnp.mean(zc * zc, axis=-1, keepdims=True)
    o_ref[0] = zc * lax.rsqrt(var + LN_EPS) * lng_ref[...] + lnb_ref[...]


def _out_projection(o_a, o_b, g, x, mod3, w_out_bf, ln_g, ln_b):
    B, S, D = x.shape
    tm = OUT_ROWS
    att_spec = pl.BlockSpec((1, N_GROUPS, tm, LANES), lambda b, s: (b, 0, s, 0))
    row_spec = pl.BlockSpec((1, tm, D), lambda b, s: (b, s, 0))
    vec_spec = pl.BlockSpec((1, D), lambda b, s: (0, 0))
    return pl.pallas_call(
        _out_kernel,
        grid=(B, S // tm),
        in_specs=[att_spec, att_spec,
                  pl.BlockSpec((1, tm, 2 * D_BRANCH), lambda b, s: (b, s, 0)),
                  row_spec,
                  pl.BlockSpec((1, 3, D), lambda b, s: (b, 0, 0)),
                  pl.BlockSpec(w_out_bf.shape, lambda b, s: (0, 0)),
                  vec_spec, vec_spec],
        out_specs=row_spec,
        out_shape=jax.ShapeDtypeStruct((B, S, D), f32),
        scratch_shapes=[pltpu.VMEM((tm, 2 * D_BRANCH), bf16)],
        compiler_params=pltpu.CompilerParams(
            dimension_semantics=("arbitrary", "arbitrary"), vmem_limit_bytes=VMEM_LIMIT),
        name="out_proj_ln",
    )(o_a, o_b, g, x, mod3, w_out_bf, ln_g.reshape(1, D), ln_b.reshape(1, D))


def _rope_tables(S):
    half = HEAD_DIM // 2
    inv = ROPE_THETA ** (-jnp.arange(half, dtype=f32) / half)
    ang = jnp.arange(S, dtype=jnp.int32).astype(f32)[:, None] * inv[None, :]
    cos, sin = jnp.cos(ang), jnp.sin(ang)
    cos_head = jnp.concatenate([cos, cos], axis=-1)
    sin_head = jnp.concatenate([-sin, sin], axis=-1)
    return (jnp.tile(cos_head, (1, HEADS_PER_GROUP)), jnp.tile(sin_head, (1, HEADS_PER_GROUP)))


def kernel(x, c, w_in, w_out, w_ada, b_ada, ln_g, ln_b):
    B, S, D = x.shape
    cos_t, sin_t = _rope_tables(S)
    for layer in range(w_in.shape[0]):
        mod3 = _modulation(c, w_ada[layer], b_ada[layer]).reshape(B, 3, D)
        qa, ka, va, qb, kb, vb, g = _projection(x, mod3, cos_t, sin_t, w_in[layer].astype(bf16))
        o_a = _dilated_attention(qa, ka, va)
        o_b = _moba_attention(qb, kb, vb)
        x = _out_projection(o_a, o_b, g, x, mod3, w_out[layer].astype(bf16), ln_g[layer], ln_b[layer])
    return x
```

```python
import numpy as np
import jax
import jax.numpy as jnp
from jax import lax
from jax.experimental import pallas as pl
from jax.experimental.pallas import tpu as pltpu

HEAD_DIM = 64
N_HEADS = 8
D_BRANCH = N_HEADS * HEAD_DIM
DILATED_PATTERNS = ((128, 1), (512, 4), (2048, 16))
MOBA_BLOCK = 256
MOBA_TOPK = 3
ROPE_THETA = 10000.0
LN_EPS = 1e-5
NEG_INF = -1e30
DEPTH = 1
DEEPNORM_ALPHA = (2.0 * DEPTH) ** 0.25
LOG2E = 1.4426950408889634
Q_SCALE = HEAD_DIM ** -0.5 * LOG2E

LANES = 128
HEADS_PER_GROUP = LANES // HEAD_DIM
N_GROUPS = N_HEADS // HEADS_PER_GROUP
ATT_TILE = 256
N_SLOTS = 3
SCORE_CHUNK = 2
STAGE_ORDER = (("S", 0), ("E", 2), ("V", 4))
PROJ_ROWS = 512
OUT_ROWS = 1024
VMEM_LIMIT = 48 * 1024 * 1024

NT_DIMS = (((1,), (1,)), ((), ()))

f32 = jnp.float32
bf16 = jnp.bfloat16


def _split_bf16(a):
    hi = a.astype(bf16)
    return hi, (a - hi.astype(f32)).astype(bf16)


def _mod_kernel(c_ref, w_ref, b_ref, o_ref):
    c_hi, c_lo = _split_bf16(c_ref[...])
    w_hi, w_lo = _split_bf16(w_ref[...])
    dot = lambda a, b: jnp.dot(a, b, preferred_element_type=f32)
    o_ref[...] = dot(c_hi, w_hi) + (dot(c_hi, w_lo) + dot(c_lo, w_hi)) + b_ref[...]


def _modulation(c, w_ada, b_ada):
    B, D = c.shape
    N = w_ada.shape[1]
    bn = 512
    return pl.pallas_call(
        _mod_kernel,
        grid=(N // bn,),
        in_specs=[pl.BlockSpec((B, D), lambda n: (0, 0)),
                  pl.BlockSpec((D, bn), lambda n: (0, n)),
                  pl.BlockSpec((1, bn), lambda n: (0, n))],
        out_specs=pl.BlockSpec((B, bn), lambda n: (0, n)),
        out_shape=jax.ShapeDtypeStruct((B, N), f32),
        name="adaln_mod",
    )(c, w_ada, b_ada.reshape(1, N))


def _proj_kernel(x_ref, mod_ref, cos_ref, sin_ref, w_ref,
                 qa_ref, ka_ref, va_ref, qb_ref, kb_ref, vb_ref, g_ref):
    tm = x_ref.shape[1]
    shift = mod_ref[0, 0:1, :]
    scale = mod_ref[0, 1:2, :]
    h = (x_ref[0] * (1.0 + scale) + shift).astype(bf16)
    cos = cos_ref[...]
    sin = sin_ref[...]
    lane = lax.broadcasted_iota(jnp.int32, (tm, LANES), 1)
    first_half = (lane & (HEAD_DIM // 2)) == 0

    def rope(t):
        partner = jnp.where(first_half,
                            pltpu.roll(t, LANES - HEAD_DIM // 2, 1),
                            pltpu.roll(t, HEAD_DIM // 2, 1))
        return t * cos + partner * sin

    targets = (qa_ref, ka_ref, va_ref, None, qb_ref, kb_ref, vb_ref, None)
    kinds = ("q", "k", "v", "g", "q", "k", "v", "g")
    for c in range(8):
        acc = jnp.dot(h, w_ref[:, c * D_BRANCH:(c + 1) * D_BRANCH],
                      preferred_element_type=f32)
        if kinds[c] == "g":
            off = 0 if c == 3 else D_BRANCH
            g_ref[0, :, off:off + D_BRANCH] = acc.astype(bf16)
            continue
        for gi in range(N_GROUPS):
            t = acc[:, gi * LANES:(gi + 1) * LANES]
            if kinds[c] == "q":
                t = rope(t) * Q_SCALE
            elif kinds[c] == "k":
                t = rope(t)
            targets[c][0, gi] = t.astype(bf16)


def _projection(x, mod3, cos_t, sin_t, w_in_bf):
    B, S, D = x.shape
    tm = PROJ_ROWS
    n_s = S // tm
    qkv_shape = jax.ShapeDtypeStruct((B, N_GROUPS, S, LANES), bf16)
    qkv_spec = pl.BlockSpec((1, N_GROUPS, tm, LANES), lambda b, s: (b, 0, s, 0))
    return pl.pallas_call(
        _proj_kernel,
        grid=(B, n_s),
        in_specs=[pl.BlockSpec((1, tm, D), lambda b, s: (b, s, 0)),
                  pl.BlockSpec((1, 3, D), lambda b, s: (b, 0, 0)),
                  pl.BlockSpec((tm, LANES), lambda b, s: (s, 0)),
                  pl.BlockSpec((tm, LANES), lambda b, s: (s, 0)),
                  pl.BlockSpec(w_in_bf.shape, lambda b, s: (0, 0))],
        out_specs=[qkv_spec] * 6 + [pl.BlockSpec((1, tm, 2 * D_BRANCH), lambda b, s: (b, s, 0))],
        out_shape=[qkv_shape] * 6 + [jax.ShapeDtypeStruct((B, S, 2 * D_BRANCH), bf16)],
        compiler_params=pltpu.CompilerParams(
            dimension_semantics=("arbitrary", "arbitrary"), vmem_limit_bytes=VMEM_LIMIT),
        name="in_proj_rope",
    )(x, mod3, cos_t, sin_t, w_in_bf)


def _own_lanes(h, shape):
    lane = lax.broadcasted_iota(jnp.int32, shape, len(shape) - 1)
    return (lane < HEAD_DIM) if h == 0 else (lane >= HEAD_DIM)


def _own_rows(h, shape):
    row = lax.broadcasted_iota(jnp.int32, shape, 0)
    return (row < HEAD_DIM) if h == 0 else (row >= HEAD_DIM)


def _fill_v_aug_t(v_ref, vaugt_ref):
    vt = v_ref[0, 0].astype(f32).T
    for h in range(HEADS_PER_GROUP):
        vaugt_ref[h] = jnp.where(_own_rows(h, vt.shape), vt, 1.0).astype(bf16)


class _Rows:
    def __init__(self, i, h, key_bias, query_bias):
        self.i, self.h = i, h
        self.key_bias, self.query_bias = key_bias, query_bias
        self.qh = self.mrun = self.m = self.acc = None


def _score_pass(u, q_ref, k_ref, s_ref, slot):
    T = ATT_TILE
    q = q_ref[0, 0, u.i * T:(u.i + 1) * T, :]
    qh = jnp.where(_own_lanes(u.h, q.shape), q, jnp.zeros_like(q))
    mrun = None
    for j0 in range(0, u.i + 1, SCORE_CHUNK):
        js = range(j0, min(j0 + SCORE_CHUNK, u.i + 1))
        s = lax.dot_general(k_ref[0, 0, js[0] * T:(js[-1] + 1) * T, :], qh, NT_DIMS,
                            preferred_element_type=f32)
        for j in js:
            sj = s[(j - j0) * T:(j - j0 + 1) * T]
            kb = u.key_bias(j)
            if kb is not None:
                sj = sj + kb
            s_ref[slot, j * T:(j + 1) * T, :] = sj
            t = jnp.max(sj.reshape(T // 8, 8, T), axis=0)
            qb = u.query_bias(j)
            if qb is not None:
                t = t + qb
            mrun = t if mrun is None else jnp.maximum(mrun, t)
    u.m = jnp.max(mrun, axis=0, keepdims=True)


def _exp_pass(u, s_ref, p_ref, slot):
    T = ATT_TILE
    for j in range(u.i + 1):
        m = u.m
        qb = u.query_bias(j)
        if qb is not None:
            m = m - qb
        rows = slice(j * T, (j + 1) * T)
        p_ref[slot, rows, :] = jnp.exp2(s_ref[slot, rows, :] - m).astype(bf16)


def _value_pass(u, p_ref, slot, vaugt_ref):
    n = (u.i + 1) * ATT_TILE
    u.acc = jnp.dot(vaugt_ref[u.h, :, 0:n], p_ref[slot, 0:n, :],
                    preferred_element_type=f32)


def _run_rows(units, q_ref, k_ref, s_ref, p_ref, vaugt_ref, o_ref):
    T = ATT_TILE
    done = {}
    units = list(units)

    def score(idx):
        if 0 <= idx < len(units):
            _score_pass(units[idx], q_ref, k_ref, s_ref, idx % N_SLOTS)

    def exp(idx):
        if 0 <= idx < len(units):
            _exp_pass(units[idx], s_ref, p_ref, idx % N_SLOTS)

    def value(idx):
        if 0 <= idx < len(units):
            prev = units[idx]
            _value_pass(prev, p_ref, idx % N_SLOTS, vaugt_ref)
            den_row = HEAD_DIM if prev.h == 0 else 0
            done[prev.h] = prev.acc / prev.acc[den_row:den_row + 1, :]
            if len(done) == HEADS_PER_GROUP:
                out_t = jnp.where(_own_rows(0, done[0].shape), done[0], done[1])
                o_ref[0, 0, prev.i * T:(prev.i + 1) * T, :] = out_t.T.astype(o_ref.dtype)
                done.clear()

    stages = {"S": score, "E": exp, "V": value}
    for idx in range(len(units) + max(lag for _, lag in STAGE_ORDER)):
        for name, lag in STAGE_ORDER:
            stages[name](idx - lag)


def _dilated_kernel(q_ref, k_ref, v_ref, slab_ref, o_ref, s_ref, p_ref, vaugt_ref):
    T = ATT_TILE
    n_t = q_ref.shape[2] // T
    _fill_v_aug_t(v_ref, vaugt_ref)
    units = [_Rows(i, h,
                   lambda j, i=i: slab_ref[(n_t - 1 - i + j) * T:(n_t - i + j) * T, :],
                   lambda j: None)
             for i in range(n_t) for h in range(HEADS_PER_GROUP)]
    _run_rows(units, q_ref, k_ref, s_ref, p_ref, vaugt_ref, o_ref)


def _moba_kernel(q_ref, k_ref, v_ref, o_ref, s_ref, p_ref, vaugt_ref, kmean_ref):
    T = ATT_TILE
    S = q_ref.shape[2]
    n_t = S // T
    _fill_v_aug_t(v_ref, vaugt_ref)

    kmean_ref[...] = jnp.zeros_like(kmean_ref)
    for j in range(n_t):
        kj = k_ref[0, 0, j * T:(j + 1) * T, :]
        kmean_ref[j:j + 1, :] = jnp.sum(kj.astype(f32), axis=0, keepdims=True) * (1.0 / T)
    kmean = kmean_ref[...]
    km_parts = []
    for h in range(HEADS_PER_GROUP):
        km_h = jnp.where(_own_lanes(h, kmean.shape), kmean, 0.0)
        km_hi = km_h.astype(bf16)
        km_parts += [km_hi, (km_h - km_hi.astype(f32)).astype(bf16)]
    km_rows = kmean.shape[0]
    c0 = min((MOBA_TOPK + 1) * T, S - T)
    W = S - c0
    gates = lax.dot_general(jnp.concatenate(km_parts, axis=0), q_ref[0, 0, c0:S, :], NT_DIMS,
                            preferred_element_type=f32)

    key_i = lax.broadcasted_iota(jnp.int32, (T, T), 0)
    qry_i = lax.broadcasted_iota(jnp.int32, (T, T), 1)
    causal_bias = jnp.where(key_i <= qry_i, 0.0, NEG_INF).astype(f32)

    blk = lax.broadcasted_iota(jnp.int32, (8, W), 0)
    own_blk = (lax.broadcasted_iota(jnp.int32, (8, W), 1) + c0) // T
    past = blk < own_blk
    drop = []
    for h in range(HEADS_PER_GROUP):
        hi0 = 2 * h * km_rows
        gate = gates[hi0:hi0 + 8] + gates[hi0 + km_rows:hi0 + km_rows + 8]
        cnt = jnp.zeros((8, W), f32)
        for jp in range(n_t - 1):
            gj = gate[jp:jp + 1, :]
            ahead = (gj > gate) | ((gj == gate) & (jp < blk))
            cnt = cnt + jnp.where(ahead & (jp < own_blk), 1.0, 0.0)
        drop.append(jnp.where(past & (cnt >= float(MOBA_TOPK)), NEG_INF, 0.0))

    def query_bias(i, h, j):
        if j == i or i * T < c0:
            return None
        return drop[h][j:j + 1, i * T - c0:(i + 1) * T - c0]

    units = [_Rows(i, h,
                   lambda j, i=i: causal_bias if j == i else None,
                   lambda j, i=i, h=h: query_bias(i, h, j))
             for i in range(n_t) for h in range(HEADS_PER_GROUP)]
    _run_rows(units, q_ref, k_ref, s_ref, p_ref, vaugt_ref, o_ref)


def _dilated_bias_slab(n_t):
    T = ATT_TILE
    key = np.arange(T)[:, None]
    qry = np.arange(T)[None, :]
    slab = np.empty((n_t * T, T), np.float32)
    for b in range(n_t):
        dist = (n_t - 1 - b) * T + qry - key
        mult = np.zeros((T, T), np.float64)
        for window, dil in DILATED_PATTERNS:
            mult += (dist >= 0) & (dist <= window) & (dist % dil == 0)
        slab[b * T:(b + 1) * T, :] = np.where(mult > 0, np.log2(np.maximum(mult, 1.0)), NEG_INF)
    return jnp.asarray(slab)


def _attention_call(body, name, q, k, v, consts, extra_scratch):
    B, G, S, _ = q.shape
    blk = pl.BlockSpec((1, 1, S, LANES), lambda b, g: (b, g, 0, 0))
    scratch = [pltpu.VMEM((N_SLOTS, S, ATT_TILE), f32),
               pltpu.VMEM((N_SLOTS, S, ATT_TILE), bf16),
               pltpu.VMEM((HEADS_PER_GROUP, LANES, S), bf16)]
    return pl.pallas_call(
        body,
        grid=(B, G),
        in_specs=[blk, blk, blk] + [pl.BlockSpec(c.shape, lambda b, g: (0, 0)) for c in consts],
        out_specs=blk,
        out_shape=jax.ShapeDtypeStruct(q.shape, bf16),
        scratch_shapes=scratch + extra_scratch,
        compiler_params=pltpu.CompilerParams(
            dimension_semantics=("arbitrary", "arbitrary"), vmem_limit_bytes=VMEM_LIMIT),
        name=name,
    )(q, k, v, *consts)


def _dilated_attention(q, k, v):
    S = q.shape[2]
    return _attention_call(_dilated_kernel, "dilated_attn", q, k, v,
                           [_dilated_bias_slab(S // ATT_TILE)], [])


def _moba_attention(q, k, v):
    S = q.shape[2]
    assert ATT_TILE == MOBA_BLOCK and S // ATT_TILE <= 8
    return _attention_call(_moba_kernel, "moba_attn", q, k, v, [],
                           [pltpu.VMEM((16, LANES), f32)])


def _out_kernel(oa_ref, ob_ref, g_ref, x_ref, mod_ref, w_ref, lng_ref, lnb_ref, o_ref, og_ref):
    for gi in range(N_GROUPS):
        for br, src in enumerate((oa_ref, ob_ref)):
            lo = br * D_BRANCH + gi * LANES
            half_g = 0.5 * g_ref[0, :, lo:lo + LANES].astype(f32)
            silu = half_g + half_g * jnp.tanh(half_g)
            og_ref[:, lo:lo + LANES] = (src[0, gi].astype(f32) * silu).astype(bf16)
    y = jnp.dot(og_ref[...], w_ref[...], preferred_element_type=f32)
    gate = mod_ref[0, 2:3, :]
    z = DEEPNORM_ALPHA * x_ref[0] + gate * y
    mu = jnp.mean(z, axis=-1, keepdims=True)
    zc = z - mu
    var = jnp.mean(zc * zc, axis=-1, keepdims=True)
    o_ref[0] = zc * lax.rsqrt(var + LN_EPS) * lng_ref[...] + lnb_ref[...]


def _out_projection(o_a, o_b, g, x, mod3, w_out_bf, ln_g, ln_b):
    B, S, D = x.shape
    tm = OUT_ROWS
    att_spec = pl.BlockSpec((1, N_GROUPS, tm, LANES), lambda b, s: (b, 0, s, 0))
    row_spec = pl.BlockSpec((1, tm, D), lambda b, s: (b, s, 0))
    vec_spec = pl.BlockSpec((1, D), lambda b, s: (0, 0))
    return pl.pallas_call(
        _out_kernel,
        grid=(B, S // tm),
        in_specs=[att_spec, att_spec,
                  pl.BlockSpec((1, tm, 2 * D_BRANCH), lambda b, s: (b, s, 0)),
                  row_spec,
                  pl.BlockSpec((1, 3, D), lambda b, s: (b, 0, 0)),
                  pl.BlockSpec(w_out_bf.shape, lambda b, s: (0, 0)),
                  vec_spec, vec_spec],
        out_specs=row_spec,
        out_shape=jax.ShapeDtypeStruct((B, S, D), f32),
        scratch_shapes=[pltpu.VMEM((tm, 2 * D_BRANCH), bf16)],
        compiler_params=pltpu.CompilerParams(
            dimension_semantics=("arbitrary", "arbitrary"), vmem_limit_bytes=VMEM_LIMIT),
        name="out_proj_ln",
    )(o_a, o_b, g, x, mod3, w_out_bf, ln_g.reshape(1, D), ln_b.reshape(1, D))


def _rope_tables(S):
    half = HEAD_DIM // 2
    inv = ROPE_THETA ** (-jnp.arange(half, dtype=f32) / half)
    ang = jnp.arange(S, dtype=jnp.int32).astype(f32)[:, None] * inv[None, :]
    cos, sin = jnp.cos(ang), jnp.sin(ang)
    cos_head = jnp.concatenate([cos, cos], axis=-1)
    sin_head = jnp.concatenate([-sin, sin], axis=-1)
    return (jnp.tile(cos_head, (1, HEADS_PER_GROUP)), jnp.tile(sin_head, (1, HEADS_PER_GROUP)))


def kernel(x, c, w_in, w_out, w_ada, b_ada, ln_g, ln_b):
    B, S, D = x.shape
    cos_t, sin_t = _rope_tables(S)
    for layer in range(w_in.shape[0]):
        mod3 = _modulation(c, w_ada[layer], b_ada[layer]).reshape(B, 3, D)
        qa, ka, va, qb, kb, vb, g = _projection(x, mod3, cos_t, sin_t, w_in[layer].astype(bf16))
        o_a = _dilated_attention(qa, ka, va)
        o_b = _moba_attention(qb, kb, vb)
        x = _out_projection(o_a, o_b, g, x, mod3, w_out[layer].astype(bf16), ln_g[layer], ln_b[layer])
    return x
```

```python
import functools

import numpy as np
import jax
import jax.numpy as jnp
from jax import lax
from jax.experimental import pallas as pl
from jax.experimental.pallas import tpu as pltpu

HEAD_DIM = 64
N_HEADS = 8
D_BRANCH = N_HEADS * HEAD_DIM
DILATED_PATTERNS = ((128, 1), (512, 4), (2048, 16))
MOBA_BLOCK = 256
MOBA_TOPK = 3
ROPE_THETA = 10000.0
LN_EPS = 1e-5
NEG_INF = -1e30
DEPTH = 1
DEEPNORM_ALPHA = (2.0 * DEPTH) ** 0.25
LOG2E = 1.4426950408889634
Q_SCALE = HEAD_DIM ** -0.5 * LOG2E

LANES = 128
HEADS_PER_GROUP = LANES // HEAD_DIM
N_GROUPS = N_HEADS // HEADS_PER_GROUP
ATT_TILE = 256
N_SLOTS = 5
SCORE_CHUNK = 2
SLAB_PERIOD = 16
STAGE_ORDER = (("S", 0), ("E", 4), ("V", 8))
PROJ_ROWS = 512
OUT_ROWS = 1024
VMEM_LIMIT = 48 * 1024 * 1024

NT_DIMS = (((1,), (1,)), ((), ()))

f32 = jnp.float32
bf16 = jnp.bfloat16


def _split_bf16(a):
    hi = a.astype(bf16)
    return hi, (a - hi.astype(f32)).astype(bf16)


def _mod_kernel(c_ref, w_ref, b_ref, o_ref):
    c_hi, c_lo = _split_bf16(c_ref[...])
    w_hi, w_lo = _split_bf16(w_ref[...])
    dot = lambda a, b: jnp.dot(a, b, preferred_element_type=f32)
    o_ref[...] = dot(c_hi, w_hi) + (dot(c_hi, w_lo) + dot(c_lo, w_hi)) + b_ref[...]


def _modulation(c, w_ada, b_ada):
    B, D = c.shape
    N = w_ada.shape[1]
    bn = 512
    return pl.pallas_call(
        _mod_kernel,
        grid=(N // bn,),
        in_specs=[pl.BlockSpec((B, D), lambda n: (0, 0)),
                  pl.BlockSpec((D, bn), lambda n: (0, n)),
                  pl.BlockSpec((1, bn), lambda n: (0, n))],
        out_specs=pl.BlockSpec((B, bn), lambda n: (0, n)),
        out_shape=jax.ShapeDtypeStruct((B, N), f32),
        name="adaln_mod",
    )(c, w_ada, b_ada.reshape(1, N))


def _proj_kernel(x_ref, mod_ref, cos_ref, sin_ref, w_ref,
                 qa_ref, ka_ref, va_ref, qb_ref, kb_ref, vb_ref, g_ref):
    tm = x_ref.shape[1]
    shift = mod_ref[0, 0:1, :]
    scale = mod_ref[0, 1:2, :]
    h = (x_ref[0] * (1.0 + scale) + shift).astype(bf16)
    cos = cos_ref[...]
    sin = sin_ref[...]
    lane = lax.broadcasted_iota(jnp.int32, (tm, LANES), 1)
    first_half = (lane & (HEAD_DIM // 2)) == 0

    def rope(t):
        partner = jnp.where(first_half,
                            pltpu.roll(t, LANES - HEAD_DIM // 2, 1),
                            pltpu.roll(t, HEAD_DIM // 2, 1))
        return t * cos + partner * sin

    targets = (qa_ref, ka_ref, va_ref, None, qb_ref, kb_ref, vb_ref, None)
    kinds = ("q", "k", "v", "g", "q", "k", "v", "g")
    for c in range(8):
        acc = jnp.dot(h, w_ref[:, c * D_BRANCH:(c + 1) * D_BRANCH],
                      preferred_element_type=f32)
        if kinds[c] == "g":
            off = 0 if c == 3 else D_BRANCH
            g_ref[0, :, off:off + D_BRANCH] = acc.astype(bf16)
            continue
        for gi in range(N_GROUPS):
            t = acc[:, gi * LANES:(gi + 1) * LANES]
            if kinds[c] == "q":
                t = rope(t) * Q_SCALE
            elif kinds[c] == "k":
                t = rope(t)
            targets[c][0, gi] = t.astype(bf16)


def _projection(x, mod3, cos_t, sin_t, w_in_bf):
    B, S, D = x.shape
    tm = PROJ_ROWS
    n_s = S // tm
    qkv_shape = jax.ShapeDtypeStruct((B, N_GROUPS, S, LANES), bf16)
    qkv_spec = pl.BlockSpec((1, N_GROUPS, tm, LANES), lambda b, s: (b, 0, s, 0))
    return pl.pallas_call(
        _proj_kernel,
        grid=(B, n_s),
        in_specs=[pl.BlockSpec((1, tm, D), lambda b, s: (b, s, 0)),
                  pl.BlockSpec((1, 3, D), lambda b, s: (b, 0, 0)),
                  pl.BlockSpec((tm, LANES), lambda b, s: (s, 0)),
                  pl.BlockSpec((tm, LANES), lambda b, s: (s, 0)),
                  pl.BlockSpec(w_in_bf.shape, lambda b, s: (0, 0))],
        out_specs=[qkv_spec] * 6 + [pl.BlockSpec((1, tm, 2 * D_BRANCH), lambda b, s: (b, s, 0))],
        out_shape=[qkv_shape] * 6 + [jax.ShapeDtypeStruct((B, S, 2 * D_BRANCH), bf16)],
        compiler_params=pltpu.CompilerParams(
            dimension_semantics=("arbitrary", "arbitrary"), vmem_limit_bytes=VMEM_LIMIT),
        name="in_proj_rope",
    )(x, mod3, cos_t, sin_t, w_in_bf)


def _own_lanes(h, shape):
    lane = lax.broadcasted_iota(jnp.int32, shape, len(shape) - 1)
    return (lane < HEAD_DIM) if h == 0 else (lane >= HEAD_DIM)


def _own_rows(h, shape):
    row = lax.broadcasted_iota(jnp.int32, shape, 0)
    return (row < HEAD_DIM) if h == 0 else (row >= HEAD_DIM)


def _fill_v_aug_t(v_ref, vaugt_ref):
    vt = v_ref[0, 0].astype(f32).T
    for h in range(HEADS_PER_GROUP):
        vaugt_ref[h] = jnp.where(_own_rows(h, vt.shape), vt, 1.0).astype(bf16)


class _Rows:
    def __init__(self, i, h, key_bias, query_bias, q_ref, k_ref, vaugt_ref, o_ref):
        self.i, self.h = i, h
        self.key_bias, self.query_bias = key_bias, query_bias
        self.q_ref, self.k_ref, self.vaugt_ref, self.o_ref = q_ref, k_ref, vaugt_ref, o_ref
        self.m = self.acc = None


def _score_pass(u, s_ref, slot):
    T = ATT_TILE
    q = u.q_ref[0, 0, u.i * T:(u.i + 1) * T, :]
    qh = jnp.where(_own_lanes(u.h, q.shape), q, jnp.zeros_like(q))
    mrun = None
    for j0 in range(0, u.i + 1, SCORE_CHUNK):
        js = range(j0, min(j0 + SCORE_CHUNK, u.i + 1))
        s = lax.dot_general(u.k_ref[0, 0, js[0] * T:(js[-1] + 1) * T, :], qh, NT_DIMS,
                            preferred_element_type=f32)
        for j in js:
            sj = s[(j - j0) * T:(j - j0 + 1) * T]
            kb = u.key_bias(j)
            if kb is not None:
                sj = sj + kb
            s_ref[slot, j * T:(j + 1) * T, :] = sj
            t = jnp.max(sj.reshape(T // 8, 8, T), axis=0)
            qb = u.query_bias(j)
            if qb is not None:
                t = t + qb
            mrun = t if mrun is None else jnp.maximum(mrun, t)
    u.m = jnp.max(mrun, axis=0, keepdims=True)


def _exp_pass(u, s_ref, p_ref, slot):
    T = ATT_TILE
    for j in range(u.i + 1):
        m = u.m
        qb = u.query_bias(j)
        if qb is not None:
            m = m - qb
        rows = slice(j * T, (j + 1) * T)
        p_ref[slot, rows, :] = jnp.exp2(s_ref[slot, rows, :] - m).astype(bf16)


def _value_pass(u, p_ref, slot):
    n = (u.i + 1) * ATT_TILE
    u.acc = jnp.dot(u.vaugt_ref[u.h, :, 0:n], p_ref[slot, 0:n, :],
                    preferred_element_type=f32)


def _run_rows(units, s_ref, p_ref):
    T = ATT_TILE
    done = {}
    units = list(units)

    def score(idx):
        if 0 <= idx < len(units):
            _score_pass(units[idx], s_ref, idx % N_SLOTS)

    def exp(idx):
        if 0 <= idx < len(units):
            _exp_pass(units[idx], s_ref, p_ref, idx % N_SLOTS)

    def value(idx):
        if 0 <= idx < len(units):
            prev = units[idx]
            _value_pass(prev, p_ref, idx % N_SLOTS)
            den_row = HEAD_DIM if prev.h == 0 else 0
            heads = done.setdefault((id(prev.o_ref), prev.i), {})
            heads[prev.h] = prev.acc / prev.acc[den_row:den_row + 1, :]
            if len(heads) == HEADS_PER_GROUP:
                out_t = jnp.where(_own_rows(0, heads[0].shape), heads[0], heads[1])
                prev.o_ref[0, 0, prev.i * T:(prev.i + 1) * T, :] = out_t.T.astype(prev.o_ref.dtype)
                del done[(id(prev.o_ref), prev.i)]

    stages = {"S": score, "E": exp, "V": value}
    for idx in range(len(units) + max(lag for _, lag in STAGE_ORDER)):
        for name, lag in STAGE_ORDER:
            stages[name](idx - lag)


def _dilated_units(periodic_offsets, q_ref, k_ref, slab_ref, o_ref, vaugt_ref):
    T = ATT_TILE
    n_t = q_ref.shape[2] // T

    def key_bias(i, j):
        row0 = (n_t - 1 - i + j) * T
        if i - j in periodic_offsets:
            return jnp.tile(slab_ref[row0:row0 + SLAB_PERIOD, :], (T // SLAB_PERIOD, 1))
        return slab_ref[row0:row0 + T, :]

    return [_Rows(i, h, lambda j, i=i: key_bias(i, j), lambda j: None,
                  q_ref, k_ref, vaugt_ref, o_ref)
            for i in range(n_t) for h in range(HEADS_PER_GROUP)]


def _moba_units(q_ref, k_ref, o_ref, vaugt_ref, kmean_ref):
    T = ATT_TILE
    S = q_ref.shape[2]
    n_t = S // T

    kmean_ref[...] = jnp.zeros_like(kmean_ref)
    for j in range(n_t):
        kj = k_ref[0, 0, j * T:(j + 1) * T, :]
        kmean_ref[j:j + 1, :] = jnp.sum(kj.astype(f32), axis=0, keepdims=True) * (1.0 / T)
    kmean = kmean_ref[...]
    km_parts = []
    for h in range(HEADS_PER_GROUP):
        km_parts += list(_split_bf16(jnp.where(_own_lanes(h, kmean.shape), kmean, 0.0)))
    km_rows = kmean.shape[0]
    c0 = min((MOBA_TOPK + 1) * T, S - T)
    W = S - c0
    gates = lax.dot_general(jnp.concatenate(km_parts, axis=0), q_ref[0, 0, c0:S, :], NT_DIMS,
                            preferred_element_type=f32)

    key_i = lax.broadcasted_iota(jnp.int32, (T, T), 0)
    qry_i = lax.broadcasted_iota(jnp.int32, (T, T), 1)
    causal_bias = jnp.where(key_i <= qry_i, 0.0, NEG_INF).astype(f32)

    blk = lax.broadcasted_iota(jnp.int32, (8, W), 0)
    own_blk = (lax.broadcasted_iota(jnp.int32, (8, W), 1) + c0) // T
    past = blk < own_blk
    drop = []
    for h in range(HEADS_PER_GROUP):
        hi0 = 2 * h * km_rows
        gate = gates[hi0:hi0 + 8] + gates[hi0 + km_rows:hi0 + km_rows + 8]
        cnt = jnp.zeros((8, W), f32)
        for jp in range(n_t - 1):
            gj = gate[jp:jp + 1, :]
            ahead = (gj > gate) | ((gj == gate) & (jp < blk))
            cnt = cnt + jnp.where(ahead & (jp < own_blk), 1.0, 0.0)
        drop.append(jnp.where(past & (cnt >= float(MOBA_TOPK)), NEG_INF, 0.0))

    def query_bias(i, h, j):
        if j == i or i * T < c0:
            return None
        return drop[h][j:j + 1, i * T - c0:(i + 1) * T - c0]

    return [_Rows(i, h,
                  lambda j, i=i: causal_bias if j == i else None,
                  lambda j, i=i, h=h: query_bias(i, h, j),
                  q_ref, k_ref, vaugt_ref, o_ref)
            for i in range(n_t) for h in range(HEADS_PER_GROUP)]


def _mixers_kernel(periodic_offsets, qa_ref, ka_ref, va_ref, qb_ref, kb_ref, vb_ref, slab_ref,
                   oa_ref, ob_ref, s_ref, p_ref, vaugt_a_ref, vaugt_b_ref, kmean_ref):
    _fill_v_aug_t(va_ref, vaugt_a_ref)
    _fill_v_aug_t(vb_ref, vaugt_b_ref)
    dilated = _dilated_units(periodic_offsets, qa_ref, ka_ref, slab_ref, oa_ref, vaugt_a_ref)
    moba = _moba_units(qb_ref, kb_ref, ob_ref, vaugt_b_ref, kmean_ref)
    _run_rows([u for pair in zip(dilated, moba) for u in pair], s_ref, p_ref)


def _dilated_bias_slab(n_t):
    T = ATT_TILE
    key = np.arange(T)[:, None]
    qry = np.arange(T)[None, :]
    slab = np.empty((n_t * T, T), np.float32)
    for b in range(n_t):
        dist = (n_t - 1 - b) * T + qry - key
        mult = np.zeros((T, T), np.float64)
        for window, dil in DILATED_PATTERNS:
            mult += (dist >= 0) & (dist <= window) & (dist % dil == 0)
        slab[b * T:(b + 1) * T, :] = np.where(mult > 0, np.log2(np.maximum(mult, 1.0)), NEG_INF)
    periodic = frozenset(
        n_t - 1 - b for b in range(n_t)
        if np.array_equal(slab[b * T:(b + 1) * T],
                          np.tile(slab[b * T:b * T + SLAB_PERIOD], (T // SLAB_PERIOD, 1))))
    return jnp.asarray(slab), periodic


def _mixers(qa, ka, va, qb, kb, vb):
    B, G, S, _ = qa.shape
    assert ATT_TILE == MOBA_BLOCK and S // ATT_TILE <= 8
    slab, periodic = _dilated_bias_slab(S // ATT_TILE)
    blk = pl.BlockSpec((1, 1, S, LANES), lambda b, g: (b, g, 0, 0))
    vaugt = pltpu.VMEM((HEADS_PER_GROUP, LANES, S), bf16)
    out = jax.ShapeDtypeStruct(qa.shape, bf16)
    return pl.pallas_call(
        functools.partial(_mixers_kernel, periodic),
        grid=(B, G),
        in_specs=[blk] * 6 + [pl.BlockSpec(slab.shape, lambda b, g: (0, 0))],
        out_specs=[blk, blk],
        out_shape=[out, out],
        scratch_shapes=[pltpu.VMEM((N_SLOTS, S, ATT_TILE), f32),
                        pltpu.VMEM((N_SLOTS, S, ATT_TILE), bf16),
                        vaugt, vaugt,
                        pltpu.VMEM((16, LANES), f32)],
        compiler_params=pltpu.CompilerParams(
            dimension_semantics=("arbitrary", "arbitrary"), vmem_limit_bytes=VMEM_LIMIT),
        name="mixers_attn",
    )(qa, ka, va, qb, kb, vb, slab)


def _out_kernel(oa_ref, ob_ref, g_ref, x_ref, mod_ref, w_ref, lng_ref, lnb_ref, o_ref, og_ref):
    for gi in range(N_GROUPS):
        for br, src in enumerate((oa_ref, ob_ref)):
            lo = br * D_BRANCH + gi * LANES
            half_g = 0.5 * g_ref[0, :, lo:lo + LANES].astype(f32)
            silu = half_g + half_g * jnp.tanh(half_g)
            og_ref[:, lo:lo + LANES] = (src[0, gi].astype(f32) * silu).astype(bf16)
    y = jnp.dot(og_ref[...], w_ref[...], preferred_element_type=f32)
    gate = mod_ref[0, 2:3, :]
    z = DEEPNORM_ALPHA * x_ref[0] + gate * y
    mu = jnp.mean(z, axis=-1, keepdims=True)
    zc = z - mu
    var = jnp.mean(zc * zc, axis=-1, keepdims=True)
    o_ref[0] = zc * lax.rsqrt(var + LN_EPS) * lng_ref[...] + lnb_ref[...]


def _out_projection(o_a, o_b, g, x, mod3, w_out_bf, ln_g, ln_b):
    B, S, D = x.shape
    tm = OUT_ROWS
    att_spec = pl.BlockSpec((1, N_GROUPS, tm, LANES), lambda b, s: (b, 0, s, 0))
    row_spec = pl.BlockSpec((1, tm, D), lambda b, s: (b, s, 0))
    vec_spec = pl.BlockSpec((1, D), lambda b, s: (0, 0))
    return pl.pallas_call(
        _out_kernel,
        grid=(B, S // tm),
        in_specs=[att_spec, att_spec,
                  pl.BlockSpec((1, tm, 2 * D_BRANCH), lambda b, s: (b, s, 0)),
                  row_spec,
                  pl.BlockSpec((1, 3, D), lambda b, s: (b, 0, 0)),
                  pl.BlockSpec(w_out_bf.shape, lambda b, s: (0, 0)),
                  vec_spec, vec_spec],
        out_specs=row_spec,
        out_shape=jax.ShapeDtypeStruct((B, S, D), f32),
        scratch_shapes=[pltpu.VMEM((tm, 2 * D_BRANCH), bf16)],
        compiler_params=pltpu.CompilerParams(
            dimension_semantics=("arbitrary", "arbitrary"), vmem_limit_bytes=VMEM_LIMIT),
        name="out_proj_ln",
    )(o_a, o_b, g, x, mod3, w_out_bf, ln_g.reshape(1, D), ln_b.reshape(1, D))


def _rope_tables(S):
    half = HEAD_DIM // 2
    inv = ROPE_THETA ** (-jnp.arange(half, dtype=f32) / half)
    ang = jnp.arange(S, dtype=jnp.int32).astype(f32)[:, None] * inv[None, :]
    cos, sin = jnp.cos(ang), jnp.sin(ang)
    cos_head = jnp.concatenate([cos, cos], axis=-1)
    sin_head = jnp.concatenate([-sin, sin], axis=-1)
    return (jnp.tile(cos_head, (1, HEADS_PER_GROUP)), jnp.tile(sin_head, (1, HEADS_PER_GROUP)))


def kernel(x, c, w_in, w_out, w_ada, b_ada, ln_g, ln_b):
    B, S, D = x.shape
    cos_t, sin_t = _rope_tables(S)
    for layer in range(w_in.shape[0]):
        mod3 = _modulation(c, w_ada[layer], b_ada[layer]).reshape(B, 3, D)
        qa, ka, va, qb, kb, vb, g = _projection(x, mod3, cos_t, sin_t, w_in[layer].astype(bf16))
        o_a, o_b = _mixers(qa, ka, va, qb, kb, vb)
        x = _out_projection(o_a, o_b, g, x, mod3, w_out[layer].astype(bf16), ln_g[layer], ln_b[layer])
    return x
```

```python
import functools

import numpy as np
import jax
import jax.numpy as jnp
from jax import lax
from jax.experimental import pallas as pl
from jax.experimental.pallas import tpu as pltpu

HEAD_DIM = 64
N_HEADS = 8
D_BRANCH = N_HEADS * HEAD_DIM
DILATED_PATTERNS = ((128, 1), (512, 4), (2048, 16))
MOBA_BLOCK = 256
MOBA_TOPK = 3
ROPE_THETA = 10000.0
LN_EPS = 1e-5
NEG_INF = -1e30
DEPTH = 1
DEEPNORM_ALPHA = (2.0 * DEPTH) ** 0.25
LOG2E = 1.4426950408889634
Q_SCALE = HEAD_DIM ** -0.5 * LOG2E

LANES = 128
HEADS_PER_GROUP = LANES // HEAD_DIM
N_GROUPS = N_HEADS // HEADS_PER_GROUP
ATT_TILE = 256
N_SLOTS = 5
SCORE_CHUNK = 2
SLAB_PERIOD = 16
STAGE_ORDER = (("S", 0), ("E", 4), ("V", 8))
PROJ_ROWS = 512
OUT_ROWS = 1024
OUT_SUB_ROWS = 256
VMEM_LIMIT = 48 * 1024 * 1024

NT_DIMS = (((1,), (1,)), ((), ()))

f32 = jnp.float32
bf16 = jnp.bfloat16


def _split_bf16(a):
    hi = a.astype(bf16)
    return hi, (a - hi.astype(f32)).astype(bf16)


def _mod_kernel(c_ref, w_ref, b_ref, o_ref):
    c_hi, c_lo = _split_bf16(c_ref[...])
    w_hi, w_lo = _split_bf16(w_ref[...])
    dot = lambda a, b: jnp.dot(a, b, preferred_element_type=f32)
    o_ref[...] = dot(c_hi, w_hi) + (dot(c_hi, w_lo) + dot(c_lo, w_hi)) + b_ref[...]


def _modulation(c, w_ada, b_ada):
    B, D = c.shape
    N = w_ada.shape[1]
    bn = 512
    return pl.pallas_call(
        _mod_kernel,
        grid=(N // bn,),
        in_specs=[pl.BlockSpec((B, D), lambda n: (0, 0)),
                  pl.BlockSpec((D, bn), lambda n: (0, n)),
                  pl.BlockSpec((1, bn), lambda n: (0, n))],
        out_specs=pl.BlockSpec((B, bn), lambda n: (0, n)),
        out_shape=jax.ShapeDtypeStruct((B, N), f32),
        name="adaln_mod",
    )(c, w_ada, b_ada.reshape(1, N))


def _modulated(x_ref, mod_ref):
    shift = mod_ref[0, 0:1, :]
    scale = mod_ref[0, 1:2, :]
    return (x_ref[0] * (1.0 + scale) + shift).astype(bf16)


def _proj_kernel(x_ref, mod_ref, cos_ref, sin_ref, w_ref,
                 qa_ref, ka_ref, va_ref, qb_ref, kb_ref, vb_ref):
    tm = x_ref.shape[1]
    h = _modulated(x_ref, mod_ref)
    cos = cos_ref[...]
    sin = sin_ref[...]
    lane = lax.broadcasted_iota(jnp.int32, (tm, LANES), 1)
    first_half = (lane & (HEAD_DIM // 2)) == 0

    def rope(t):
        partner = jnp.where(first_half,
                            pltpu.roll(t, LANES - HEAD_DIM // 2, 1),
                            pltpu.roll(t, HEAD_DIM // 2, 1))
        return t * cos + partner * sin

    targets = (qa_ref, ka_ref, va_ref, qb_ref, kb_ref, vb_ref)
    kinds = ("q", "k", "v", "q", "k", "v")
    for c in range(len(targets)):
        acc = jnp.dot(h, w_ref[:, c * D_BRANCH:(c + 1) * D_BRANCH],
                      preferred_element_type=f32)
        for gi in range(N_GROUPS):
            t = acc[:, gi * LANES:(gi + 1) * LANES]
            if kinds[c] == "q":
                t = rope(t) * Q_SCALE
            elif kinds[c] == "k":
                t = rope(t)
            targets[c][0, gi] = t.astype(bf16)


def _projection(x, mod3, cos_t, sin_t, w_qkv_bf):
    B, S, D = x.shape
    tm = PROJ_ROWS
    n_s = S // tm
    qkv_shape = jax.ShapeDtypeStruct((B, N_GROUPS, S, LANES), bf16)
    qkv_spec = pl.BlockSpec((1, N_GROUPS, tm, LANES), lambda b, s: (b, 0, s, 0))
    return pl.pallas_call(
        _proj_kernel,
        grid=(B, n_s),
        in_specs=[pl.BlockSpec((1, tm, D), lambda b, s: (b, s, 0)),
                  pl.BlockSpec((1, 3, D), lambda b, s: (b, 0, 0)),
                  pl.BlockSpec((tm, LANES), lambda b, s: (s, 0)),
                  pl.BlockSpec((tm, LANES), lambda b, s: (s, 0)),
                  pl.BlockSpec(w_qkv_bf.shape, lambda b, s: (0, 0))],
        out_specs=[qkv_spec] * 6,
        out_shape=[qkv_shape] * 6,
        compiler_params=pltpu.CompilerParams(
            dimension_semantics=("arbitrary", "arbitrary"), vmem_limit_bytes=VMEM_LIMIT),
        name="in_proj_rope",
    )(x, mod3, cos_t, sin_t, w_qkv_bf)


def _own_lanes(h, shape):
    lane = lax.broadcasted_iota(jnp.int32, shape, len(shape) - 1)
    return (lane < HEAD_DIM) if h == 0 else (lane >= HEAD_DIM)


def _own_rows(h, shape):
    row = lax.broadcasted_iota(jnp.int32, shape, 0)
    return (row < HEAD_DIM) if h == 0 else (row >= HEAD_DIM)


def _fill_v_aug_t(v_ref, vaugt_ref):
    vt = v_ref[0, 0].astype(f32).T
    for h in range(HEADS_PER_GROUP):
        vaugt_ref[h] = jnp.where(_own_rows(h, vt.shape), vt, 1.0).astype(bf16)


class _Rows:
    def __init__(self, i, h, key_bias, query_bias, q_ref, k_ref, vaugt_ref, o_ref):
        self.i, self.h = i, h
        self.key_bias, self.query_bias = key_bias, query_bias
        self.q_ref, self.k_ref, self.vaugt_ref, self.o_ref = q_ref, k_ref, vaugt_ref, o_ref
        self.m = self.acc = None


def _score_pass(u, s_ref, slot):
    T = ATT_TILE
    q = u.q_ref[0, 0, u.i * T:(u.i + 1) * T, :]
    qh = jnp.where(_own_lanes(u.h, q.shape), q, jnp.zeros_like(q))
    mrun = None
    for j0 in range(0, u.i + 1, SCORE_CHUNK):
        js = range(j0, min(j0 + SCORE_CHUNK, u.i + 1))
        s = lax.dot_general(u.k_ref[0, 0, js[0] * T:(js[-1] + 1) * T, :], qh, NT_DIMS,
                            preferred_element_type=f32)
        for j in js:
            sj = s[(j - j0) * T:(j - j0 + 1) * T]
            kb = u.key_bias(j)
            if kb is not None:
                sj = sj + kb
            s_ref[slot, j * T:(j + 1) * T, :] = sj
            t = jnp.max(sj.reshape(T // 8, 8, T), axis=0)
            qb = u.query_bias(j)
            if qb is not None:
                t = t + qb
            mrun = t if mrun is None else jnp.maximum(mrun, t)
    u.m = jnp.max(mrun, axis=0, keepdims=True)


def _exp_pass(u, s_ref, p_ref, slot):
    T = ATT_TILE
    for j in range(u.i + 1):
        m = u.m
        qb = u.query_bias(j)
        if qb is not None:
            m = m - qb
        rows = slice(j * T, (j + 1) * T)
        p_ref[slot, rows, :] = jnp.exp2(s_ref[slot, rows, :] - m).astype(bf16)


def _value_pass(u, p_ref, slot):
    n = (u.i + 1) * ATT_TILE
    u.acc = jnp.dot(u.vaugt_ref[u.h, :, 0:n], p_ref[slot, 0:n, :],
                    preferred_element_type=f32)


def _run_rows(units, s_ref, p_ref):
    T = ATT_TILE
    done = {}
    units = list(units)

    def score(idx):
        if 0 <= idx < len(units):
            _score_pass(units[idx], s_ref, idx % N_SLOTS)

    def exp(idx):
        if 0 <= idx < len(units):
            _exp_pass(units[idx], s_ref, p_ref, idx % N_SLOTS)

    def value(idx):
        if 0 <= idx < len(units):
            prev = units[idx]
            _value_pass(prev, p_ref, idx % N_SLOTS)
            den_row = HEAD_DIM if prev.h == 0 else 0
            heads = done.setdefault((id(prev.o_ref), prev.i), {})
            heads[prev.h] = prev.acc / prev.acc[den_row:den_row + 1, :]
            if len(heads) == HEADS_PER_GROUP:
                out_t = jnp.where(_own_rows(0, heads[0].shape), heads[0], heads[1])
                prev.o_ref[0, 0, prev.i * T:(prev.i + 1) * T, :] = out_t.T.astype(prev.o_ref.dtype)
                del done[(id(prev.o_ref), prev.i)]

    stages = {"S": score, "E": exp, "V": value}
    for idx in range(len(units) + max(lag for _, lag in STAGE_ORDER)):
        for name, lag in STAGE_ORDER:
            stages[name](idx - lag)


def _dilated_units(periodic_offsets, q_ref, k_ref, slab_ref, o_ref, vaugt_ref):
    T = ATT_TILE
    n_t = q_ref.shape[2] // T

    def key_bias(i, j):
        row0 = (n_t - 1 - i + j) * T
        if i - j in periodic_offsets:
            return jnp.tile(slab_ref[row0:row0 + SLAB_PERIOD, :], (T // SLAB_PERIOD, 1))
        return slab_ref[row0:row0 + T, :]

    return [_Rows(i, h, lambda j, i=i: key_bias(i, j), lambda j: None,
                  q_ref, k_ref, vaugt_ref, o_ref)
            for i in range(n_t) for h in range(HEADS_PER_GROUP)]


def _moba_units(q_ref, k_ref, o_ref, vaugt_ref, kmean_ref):
    T = ATT_TILE
    S = q_ref.shape[2]
    n_t = S // T

    kmean_ref[...] = jnp.zeros_like(kmean_ref)
    for j in range(n_t):
        kj = k_ref[0, 0, j * T:(j + 1) * T, :]
        kmean_ref[j:j + 1, :] = jnp.sum(kj.astype(f32), axis=0, keepdims=True) * (1.0 / T)
    kmean = kmean_ref[...]
    km_parts = []
    for h in range(HEADS_PER_GROUP):
        km_parts += list(_split_bf16(jnp.where(_own_lanes(h, kmean.shape), kmean, 0.0)))
    km_rows = kmean.shape[0]
    c0 = min((MOBA_TOPK + 1) * T, S - T)
    W = S - c0
    gates = lax.dot_general(jnp.concatenate(km_parts, axis=0), q_ref[0, 0, c0:S, :], NT_DIMS,
                            preferred_element_type=f32)

    key_i = lax.broadcasted_iota(jnp.int32, (T, T), 0)
    qry_i = lax.broadcasted_iota(jnp.int32, (T, T), 1)
    causal_bias = jnp.where(key_i <= qry_i, 0.0, NEG_INF).astype(f32)

    blk = lax.broadcasted_iota(jnp.int32, (8, W), 0)
    own_blk = (lax.broadcasted_iota(jnp.int32, (8, W), 1) + c0) // T
    past = blk < own_blk
    drop = []
    for h in range(HEADS_PER_GROUP):
        hi0 = 2 * h * km_rows
        gate = gates[hi0:hi0 + 8] + gates[hi0 + km_rows:hi0 + km_rows + 8]
        cnt = jnp.zeros((8, W), f32)
        for jp in range(n_t - 1):
            gj = gate[jp:jp + 1, :]
            ahead = (gj > gate) | ((gj == gate) & (jp < blk))
            cnt = cnt + jnp.where(ahead & (jp < own_blk), 1.0, 0.0)
        drop.append(jnp.where(past & (cnt >= float(MOBA_TOPK)), NEG_INF, 0.0))

    def query_bias(i, h, j):
        if j == i or i * T < c0:
            return None
        return drop[h][j:j + 1, i * T - c0:(i + 1) * T - c0]

    return [_Rows(i, h,
                  lambda j, i=i: causal_bias if j == i else None,
                  lambda j, i=i, h=h: query_bias(i, h, j),
                  q_ref, k_ref, vaugt_ref, o_ref)
            for i in range(n_t) for h in range(HEADS_PER_GROUP)]


def _mixers_kernel(periodic_offsets, qa_ref, ka_ref, va_ref, qb_ref, kb_ref, vb_ref, slab_ref,
                   oa_ref, ob_ref, s_ref, p_ref, vaugt_a_ref, vaugt_b_ref, kmean_ref):
    _fill_v_aug_t(va_ref, vaugt_a_ref)
    _fill_v_aug_t(vb_ref, vaugt_b_ref)
    dilated = _dilated_units(periodic_offsets, qa_ref, ka_ref, slab_ref, oa_ref, vaugt_a_ref)
    moba = _moba_units(qb_ref, kb_ref, ob_ref, vaugt_b_ref, kmean_ref)
    _run_rows([u for pair in zip(dilated, moba) for u in pair], s_ref, p_ref)


def _dilated_bias_slab(n_t):
    T = ATT_TILE
    key = np.arange(T)[:, None]
    qry = np.arange(T)[None, :]
    slab = np.empty((n_t * T, T), np.float32)
    for b in range(n_t):
        dist = (n_t - 1 - b) * T + qry - key
        mult = np.zeros((T, T), np.float64)
        for window, dil in DILATED_PATTERNS:
            mult += (dist >= 0) & (dist <= window) & (dist % dil == 0)
        slab[b * T:(b + 1) * T, :] = np.where(mult > 0, np.log2(np.maximum(mult, 1.0)), NEG_INF)
    periodic = frozenset(
        n_t - 1 - b for b in range(n_t)
        if np.array_equal(slab[b * T:(b + 1) * T],
                          np.tile(slab[b * T:b * T + SLAB_PERIOD], (T // SLAB_PERIOD, 1))))
    return jnp.asarray(slab), periodic


def _mixers(qa, ka, va, qb, kb, vb):
    B, G, S, _ = qa.shape
    assert ATT_TILE == MOBA_BLOCK and S // ATT_TILE <= 8
    slab, periodic = _dilated_bias_slab(S // ATT_TILE)
    blk = pl.BlockSpec((1, 1, S, LANES), lambda b, g: (b, g, 0, 0))
    vaugt = pltpu.VMEM((HEADS_PER_GROUP, LANES, S), bf16)
    out = jax.ShapeDtypeStruct(qa.shape, bf16)
    return pl.pallas_call(
        functools.partial(_mixers_kernel, periodic),
        grid=(B, G),
        in_specs=[blk] * 6 + [pl.BlockSpec(slab.shape, lambda b, g: (0, 0))],
        out_specs=[blk, blk],
        out_shape=[out, out],
        scratch_shapes=[pltpu.VMEM((N_SLOTS, S, ATT_TILE), f32),
                        pltpu.VMEM((N_SLOTS, S, ATT_TILE), bf16),
                        vaugt, vaugt,
                        pltpu.VMEM((16, LANES), f32)],
        compiler_params=pltpu.CompilerParams(
            dimension_semantics=("arbitrary", "arbitrary"), vmem_limit_bytes=VMEM_LIMIT),
        name="mixers_attn",
    )(qa, ka, va, qb, kb, vb, slab)


def _out_kernel(oa_ref, ob_ref, x_ref, mod_ref, wg_ref, w_ref, lng_ref, lnb_ref, o_ref, og_ref):
    tm = x_ref.shape[1]
    shift, scale, gate = mod_ref[0, 0:1, :], mod_ref[0, 1:2, :], mod_ref[0, 2:3, :]

    def gate_stage(rows):
        h = (x_ref[0, rows, :] * (1.0 + scale) + shift).astype(bf16)
        for br, src in enumerate((oa_ref, ob_ref)):
            half_g = 0.5 * jnp.dot(h, wg_ref[:, br * D_BRANCH:(br + 1) * D_BRANCH],
                                   preferred_element_type=f32)
            silu = half_g + half_g * jnp.tanh(half_g)
            for gi in range(N_GROUPS):
                lo = br * D_BRANCH + gi * LANES
                og_ref[rows, lo:lo + LANES] = (src[0, gi, rows, :].astype(f32)
                                               * silu[:, gi * LANES:(gi + 1) * LANES]).astype(bf16)

    def out_stage(rows):
        y = jnp.dot(og_ref[rows, :], w_ref[...], preferred_element_type=f32)
        z = DEEPNORM_ALPHA * x_ref[0, rows, :] + gate * y
        mu = jnp.mean(z, axis=-1, keepdims=True)
        zc = z - mu
        var = jnp.mean(zc * zc, axis=-1, keepdims=True)
        o_ref[0, rows, :] = zc * lax.rsqrt(var + LN_EPS) * lng_ref[...] + lnb_ref[...]

    subs = [slice(r, r + OUT_SUB_ROWS) for r in range(0, tm, OUT_SUB_ROWS)]
    for idx in range(len(subs) + 1):
        if idx < len(subs):
            gate_stage(subs[idx])
        if idx >= 1:
            out_stage(subs[idx - 1])


def _out_projection(o_a, o_b, x, mod3, w_gate_bf, w_out_bf, ln_g, ln_b):
    B, S, D = x.shape
    tm = OUT_ROWS
    att_spec = pl.BlockSpec((1, N_GROUPS, tm, LANES), lambda b, s: (b, 0, s, 0))
    row_spec = pl.BlockSpec((1, tm, D), lambda b, s: (b, s, 0))
    vec_spec = pl.BlockSpec((1, D), lambda b, s: (0, 0))
    return pl.pallas_call(
        _out_kernel,
        grid=(B, S // tm),
        in_specs=[att_spec, att_spec,
                  row_spec,
                  pl.BlockSpec((1, 3, D), lambda b, s: (b, 0, 0)),
                  pl.BlockSpec(w_gate_bf.shape, lambda b, s: (0, 0)),
                  pl.BlockSpec(w_out_bf.shape, lambda b, s: (0, 0)),
                  vec_spec, vec_spec],
        out_specs=row_spec,
        out_shape=jax.ShapeDtypeStruct((B, S, D), f32),
        scratch_shapes=[pltpu.VMEM((tm, 2 * D_BRANCH), bf16)],
        compiler_params=pltpu.CompilerParams(
            dimension_semantics=("arbitrary", "arbitrary"), vmem_limit_bytes=VMEM_LIMIT),
        name="out_proj_ln",
    )(o_a, o_b, x, mod3, w_gate_bf, w_out_bf, ln_g.reshape(1, D), ln_b.reshape(1, D))


def _rope_tables(S):
    half = HEAD_DIM // 2
    inv = ROPE_THETA ** (-jnp.arange(half, dtype=f32) / half)
    ang = jnp.arange(S, dtype=jnp.int32).astype(f32)[:, None] * inv[None, :]
    cos, sin = jnp.cos(ang), jnp.sin(ang)
    cos_head = jnp.concatenate([cos, cos], axis=-1)
    sin_head = jnp.concatenate([-sin, sin], axis=-1)
    return (jnp.tile(cos_head, (1, HEADS_PER_GROUP)), jnp.tile(sin_head, (1, HEADS_PER_GROUP)))


def kernel(x, c, w_in, w_out, w_ada, b_ada, ln_g, ln_b):
    B, S, D = x.shape
    cos_t, sin_t = _rope_tables(S)
    for layer in range(w_in.shape[0]):
        mod3 = _modulation(c, w_ada[layer], b_ada[layer]).reshape(B, 3, D)
        w = w_in[layer].astype(bf16).reshape(D, 2, 4, D_BRANCH)
        w_qkv = w[:, :, 0:3, :].reshape(D, 6 * D_BRANCH)
        w_gate = w[:, :, 3, :].reshape(D, 2 * D_BRANCH)
        qa, ka, va, qb, kb, vb = _projection(x, mod3, cos_t, sin_t, w_qkv)
        o_a, o_b = _mixers(qa, ka, va, qb, kb, vb)
        x = _out_projection(o_a, o_b, x, mod3, w_gate, w_out[layer].astype(bf16),
                            ln_g[layer], ln_b[layer])
    return x
```

```python
import functools

import numpy as np
import jax
import jax.numpy as jnp
from jax import lax
from jax.experimental import pallas as pl
from jax.experimental.pallas import tpu as pltpu

HEAD_DIM = 64
N_HEADS = 8
D_BRANCH = N_HEADS * HEAD_DIM
DILATED_PATTERNS = ((128, 1), (512, 4), (2048, 16))
MOBA_BLOCK = 256
MOBA_TOPK = 3
ROPE_THETA = 10000.0
LN_EPS = 1e-5
NEG_INF = -1e30
DEPTH = 1
DEEPNORM_ALPHA = (2.0 * DEPTH) ** 0.25
LOG2E = 1.4426950408889634
Q_SCALE = HEAD_DIM ** -0.5 * LOG2E

LANES = 128
HEADS_PER_GROUP = LANES // HEAD_DIM
N_GROUPS = N_HEADS // HEADS_PER_GROUP
ATT_TILE = 256
N_SLOTS = 5
SCORE_CHUNK = 2
SLAB_PERIOD = 16
STAGE_ORDER = (("S", 0), ("E", 4), ("V", 8))
PROJ_ROWS = 512
OUT_ROWS = 1024
OUT_SUB_ROWS = 256
VMEM_LIMIT = 48 * 1024 * 1024

NT_DIMS = (((1,), (1,)), ((), ()))

f32 = jnp.float32
bf16 = jnp.bfloat16


def _split_bf16(a):
    hi = a.astype(bf16)
    return hi, (a - hi.astype(f32)).astype(bf16)


def _mod_kernel(c_ref, w_ref, b_ref, o_ref):
    c_hi, c_lo = _split_bf16(c_ref[...])
    w_hi, w_lo = _split_bf16(w_ref[...])
    dot = lambda a, b: jnp.dot(a, b, preferred_element_type=f32)
    o_ref[...] = dot(c_hi, w_hi) + (dot(c_hi, w_lo) + dot(c_lo, w_hi)) + b_ref[...]


def _modulation(c, w_ada, b_ada):
    B, D = c.shape
    N = w_ada.shape[1]
    bn = 512
    return pl.pallas_call(
        _mod_kernel,
        grid=(N // bn,),
        in_specs=[pl.BlockSpec((B, D), lambda n: (0, 0)),
                  pl.BlockSpec((D, bn), lambda n: (0, n)),
                  pl.BlockSpec((1, bn), lambda n: (0, n))],
        out_specs=pl.BlockSpec((B, bn), lambda n: (0, n)),
        out_shape=jax.ShapeDtypeStruct((B, N), f32),
        name="adaln_mod",
    )(c, w_ada, b_ada.reshape(1, N))


def _modulated(x_ref, mod_ref):
    shift = mod_ref[0, 0:1, :]
    scale = mod_ref[0, 1:2, :]
    return (x_ref[0] * (1.0 + scale) + shift).astype(bf16)


def _proj_kernel(x_ref, mod_ref, cos_ref, sin_ref, w_ref,
                 qa_ref, ka_ref, va_ref, ga_ref, qb_ref, kb_ref, vb_ref):
    tm = x_ref.shape[1]
    h = _modulated(x_ref, mod_ref)
    cos = cos_ref[...]
    sin = sin_ref[...]
    lane = lax.broadcasted_iota(jnp.int32, (tm, LANES), 1)
    first_half = (lane & (HEAD_DIM // 2)) == 0

    def rope(t):
        partner = jnp.where(first_half,
                            pltpu.roll(t, LANES - HEAD_DIM // 2, 1),
                            pltpu.roll(t, HEAD_DIM // 2, 1))
        return t * cos + partner * sin

    targets = (qa_ref, ka_ref, va_ref, ga_ref, qb_ref, kb_ref, vb_ref)
    kinds = ("q", "k", "v", "g", "q", "k", "v")
    for c in range(len(targets)):
        acc = jnp.dot(h, w_ref[:, c * D_BRANCH:(c + 1) * D_BRANCH],
                      preferred_element_type=f32)
        if kinds[c] == "g":
            ga_ref[0] = acc.astype(bf16)
            continue
        for gi in range(N_GROUPS):
            t = acc[:, gi * LANES:(gi + 1) * LANES]
            if kinds[c] == "q":
                t = rope(t) * Q_SCALE
            elif kinds[c] == "k":
                t = rope(t)
            targets[c][0, gi] = t.astype(bf16)


def _projection(x, mod3, cos_t, sin_t, w_in_bf):
    B, S, D = x.shape
    tm = PROJ_ROWS
    n_s = S // tm
    qkv_shape = jax.ShapeDtypeStruct((B, N_GROUPS, S, LANES), bf16)
    qkv_spec = pl.BlockSpec((1, N_GROUPS, tm, LANES), lambda b, s: (b, 0, s, 0))
    g_shape = jax.ShapeDtypeStruct((B, S, D_BRANCH), bf16)
    g_spec = pl.BlockSpec((1, tm, D_BRANCH), lambda b, s: (b, s, 0))
    return pl.pallas_call(
        _proj_kernel,
        grid=(B, n_s),
        in_specs=[pl.BlockSpec((1, tm, D), lambda b, s: (b, s, 0)),
                  pl.BlockSpec((1, 3, D), lambda b, s: (b, 0, 0)),
                  pl.BlockSpec((tm, LANES), lambda b, s: (s, 0)),
                  pl.BlockSpec((tm, LANES), lambda b, s: (s, 0)),
                  pl.BlockSpec((D, 7 * D_BRANCH), lambda b, s: (0, 0))],
        out_specs=[qkv_spec] * 3 + [g_spec] + [qkv_spec] * 3,
        out_shape=[qkv_shape] * 3 + [g_shape] + [qkv_shape] * 3,
        compiler_params=pltpu.CompilerParams(
            dimension_semantics=("arbitrary", "arbitrary"), vmem_limit_bytes=VMEM_LIMIT),
        name="in_proj_rope",
    )(x, mod3, cos_t, sin_t, w_in_bf)


def _own_lanes(h, shape):
    lane = lax.broadcasted_iota(jnp.int32, shape, len(shape) - 1)
    return (lane < HEAD_DIM) if h == 0 else (lane >= HEAD_DIM)


def _own_rows(h, shape):
    row = lax.broadcasted_iota(jnp.int32, shape, 0)
    return (row < HEAD_DIM) if h == 0 else (row >= HEAD_DIM)


def _fill_v_aug_t(v_ref, vaugt_ref):
    vt = v_ref[0, 0].astype(f32).T
    for h in range(HEADS_PER_GROUP):
        vaugt_ref[h] = jnp.where(_own_rows(h, vt.shape), vt, 1.0).astype(bf16)


class _Rows:
    def __init__(self, i, h, key_bias, query_bias, q_ref, k_ref, vaugt_ref, o_ref):
        self.i, self.h = i, h
        self.key_bias, self.query_bias = key_bias, query_bias
        self.q_ref, self.k_ref, self.vaugt_ref, self.o_ref = q_ref, k_ref, vaugt_ref, o_ref
        self.m = self.acc = None


def _score_pass(u, s_ref, slot):
    T = ATT_TILE
    q = u.q_ref[0, 0, u.i * T:(u.i + 1) * T, :]
    qh = jnp.where(_own_lanes(u.h, q.shape), q, jnp.zeros_like(q))
    mrun = None
    for j0 in range(0, u.i + 1, SCORE_CHUNK):
        js = range(j0, min(j0 + SCORE_CHUNK, u.i + 1))
        s = lax.dot_general(u.k_ref[0, 0, js[0] * T:(js[-1] + 1) * T, :], qh, NT_DIMS,
                            preferred_element_type=f32)
        for j in js:
            sj = s[(j - j0) * T:(j - j0 + 1) * T]
            kb = u.key_bias(j)
            if kb is not None:
                sj = sj + kb
            s_ref[slot, j * T:(j + 1) * T, :] = sj
            t = jnp.max(sj.reshape(T // 8, 8, T), axis=0)
            qb = u.query_bias(j)
            if qb is not None:
                t = t + qb
            mrun = t if mrun is None else jnp.maximum(mrun, t)
    u.m = jnp.max(mrun, axis=0, keepdims=True)


def _exp_pass(u, s_ref, p_ref, slot):
    T = ATT_TILE
    for j in range(u.i + 1):
        m = u.m
        qb = u.query_bias(j)
        if qb is not None:
            m = m - qb
        rows = slice(j * T, (j + 1) * T)
        p_ref[slot, rows, :] = jnp.exp2(s_ref[slot, rows, :] - m).astype(bf16)


def _value_pass(u, p_ref, slot):
    n = (u.i + 1) * ATT_TILE
    u.acc = jnp.dot(u.vaugt_ref[u.h, :, 0:n], p_ref[slot, 0:n, :],
                    preferred_element_type=f32)


def _run_rows(units, s_ref, p_ref):
    T = ATT_TILE
    done = {}
    units = list(units)

    def score(idx):
        if 0 <= idx < len(units):
            _score_pass(units[idx], s_ref, idx % N_SLOTS)

    def exp(idx):
        if 0 <= idx < len(units):
            _exp_pass(units[idx], s_ref, p_ref, idx % N_SLOTS)

    def value(idx):
        if 0 <= idx < len(units):
            prev = units[idx]
            _value_pass(prev, p_ref, idx % N_SLOTS)
            den_row = HEAD_DIM if prev.h == 0 else 0
            heads = done.setdefault((id(prev.o_ref), prev.i), {})
            heads[prev.h] = prev.acc / prev.acc[den_row:den_row + 1, :]
            if len(heads) == HEADS_PER_GROUP:
                out_t = jnp.where(_own_rows(0, heads[0].shape), heads[0], heads[1])
                prev.o_ref[0, 0, prev.i * T:(prev.i + 1) * T, :] = out_t.T.astype(prev.o_ref.dtype)
                del done[(id(prev.o_ref), prev.i)]

    stages = {"S": score, "E": exp, "V": value}
    for idx in range(len(units) + max(lag for _, lag in STAGE_ORDER)):
        for name, lag in STAGE_ORDER:
            stages[name](idx - lag)


def _dilated_units(periodic_offsets, q_ref, k_ref, slab_ref, o_ref, vaugt_ref):
    T = ATT_TILE
    n_t = q_ref.shape[2] // T

    def key_bias(i, j):
        row0 = (n_t - 1 - i + j) * T
        if i - j in periodic_offsets:
            return jnp.tile(slab_ref[row0:row0 + SLAB_PERIOD, :], (T // SLAB_PERIOD, 1))
        return slab_ref[row0:row0 + T, :]

    return [_Rows(i, h, lambda j, i=i: key_bias(i, j), lambda j: None,
                  q_ref, k_ref, vaugt_ref, o_ref)
            for i in range(n_t) for h in range(HEADS_PER_GROUP)]


def _moba_units(q_ref, k_ref, o_ref, vaugt_ref, kmean_ref):
    T = ATT_TILE
    S = q_ref.shape[2]
    n_t = S // T

    kmean_ref[...] = jnp.zeros_like(kmean_ref)
    for j in range(n_t):
        kj = k_ref[0, 0, j * T:(j + 1) * T, :]
        kmean_ref[j:j + 1, :] = jnp.sum(kj.astype(f32), axis=0, keepdims=True) * (1.0 / T)
    kmean = kmean_ref[...]
    km_parts = []
    for h in range(HEADS_PER_GROUP):
        km_parts += list(_split_bf16(jnp.where(_own_lanes(h, kmean.shape), kmean, 0.0)))
    km_rows = kmean.shape[0]
    c0 = min((MOBA_TOPK + 1) * T, S - T)
    W = S - c0
    gates = lax.dot_general(jnp.concatenate(km_parts, axis=0), q_ref[0, 0, c0:S, :], NT_DIMS,
                            preferred_element_type=f32)

    key_i = lax.broadcasted_iota(jnp.int32, (T, T), 0)
    qry_i = lax.broadcasted_iota(jnp.int32, (T, T), 1)
    causal_bias = jnp.where(key_i <= qry_i, 0.0, NEG_INF).astype(f32)

    blk = lax.broadcasted_iota(jnp.int32, (8, W), 0)
    own_blk = (lax.broadcasted_iota(jnp.int32, (8, W), 1) + c0) // T
    past = blk < own_blk
    drop = []
    for h in range(HEADS_PER_GROUP):
        hi0 = 2 * h * km_rows
        gate = gates[hi0:hi0 + 8] + gates[hi0 + km_rows:hi0 + km_rows + 8]
        cnt = jnp.zeros((8, W), f32)
        for jp in range(n_t - 1):
            gj = gate[jp:jp + 1, :]
            ahead = (gj > gate) | ((gj == gate) & (jp < blk))
            cnt = cnt + jnp.where(ahead & (jp < own_blk), 1.0, 0.0)
        drop.append(jnp.where(past & (cnt >= float(MOBA_TOPK)), NEG_INF, 0.0))

    def query_bias(i, h, j):
        if j == i or i * T < c0:
            return None
        return drop[h][j:j + 1, i * T - c0:(i + 1) * T - c0]

    return [_Rows(i, h,
                  lambda j, i=i: causal_bias if j == i else None,
                  lambda j, i=i, h=h: query_bias(i, h, j),
                  q_ref, k_ref, vaugt_ref, o_ref)
            for i in range(n_t) for h in range(HEADS_PER_GROUP)]


def _mixers_kernel(periodic_offsets, qa_ref, ka_ref, va_ref, qb_ref, kb_ref, vb_ref, slab_ref,
                   oa_ref, ob_ref, s_ref, p_ref, vaugt_a_ref, vaugt_b_ref, kmean_ref):
    _fill_v_aug_t(va_ref, vaugt_a_ref)
    _fill_v_aug_t(vb_ref, vaugt_b_ref)
    dilated = _dilated_units(periodic_offsets, qa_ref, ka_ref, slab_ref, oa_ref, vaugt_a_ref)
    moba = _moba_units(qb_ref, kb_ref, ob_ref, vaugt_b_ref, kmean_ref)
    _run_rows([u for pair in zip(dilated, moba) for u in pair], s_ref, p_ref)


def _dilated_bias_slab(n_t):
    T = ATT_TILE
    key = np.arange(T)[:, None]
    qry = np.arange(T)[None, :]
    slab = np.empty((n_t * T, T), np.float32)
    for b in range(n_t):
        dist = (n_t - 1 - b) * T + qry - key
        mult = np.zeros((T, T), np.float64)
        for window, dil in DILATED_PATTERNS:
            mult += (dist >= 0) & (dist <= window) & (dist % dil == 0)
        slab[b * T:(b + 1) * T, :] = np.where(mult > 0, np.log2(np.maximum(mult, 1.0)), NEG_INF)
    periodic = frozenset(
        n_t - 1 - b for b in range(n_t)
        if np.array_equal(slab[b * T:(b + 1) * T],
                          np.tile(slab[b * T:b * T + SLAB_PERIOD], (T // SLAB_PERIOD, 1))))
    return jnp.asarray(slab), periodic


def _mixers(qa, ka, va, qb, kb, vb):
    B, G, S, _ = qa.shape
    assert ATT_TILE == MOBA_BLOCK and S // ATT_TILE <= 8
    slab, periodic = _dilated_bias_slab(S // ATT_TILE)
    blk = pl.BlockSpec((1, 1, S, LANES), lambda b, g: (b, g, 0, 0))
    vaugt = pltpu.VMEM((HEADS_PER_GROUP, LANES, S), bf16)
    out = jax.ShapeDtypeStruct(qa.shape, bf16)
    return pl.pallas_call(
        functools.partial(_mixers_kernel, periodic),
        grid=(B, G),
        in_specs=[blk] * 6 + [pl.BlockSpec(slab.shape, lambda b, g: (0, 0))],
        out_specs=[blk, blk],
        out_shape=[out, out],
        scratch_shapes=[pltpu.VMEM((N_SLOTS, S, ATT_TILE), f32),
                        pltpu.VMEM((N_SLOTS, S, ATT_TILE), bf16),
                        vaugt, vaugt,
                        pltpu.VMEM((16, LANES), f32)],
        compiler_params=pltpu.CompilerParams(
            dimension_semantics=("arbitrary", "arbitrary"), vmem_limit_bytes=VMEM_LIMIT),
        name="mixers_attn",
    )(qa, ka, va, qb, kb, vb, slab)


def _out_kernel(oa_ref, ob_ref, ga_ref, x_ref, mod_ref, wgb_ref, w_ref, lng_ref, lnb_ref,
                o_ref, og_ref):
    tm = x_ref.shape[1]
    shift, scale, gate = mod_ref[0, 0:1, :], mod_ref[0, 1:2, :], mod_ref[0, 2:3, :]

    def gate_stage(rows):
        h = (x_ref[0, rows, :] * (1.0 + scale) + shift).astype(bf16)
        g_b = jnp.dot(h, wgb_ref[...], preferred_element_type=f32)
        for br, (src, g) in enumerate(((oa_ref, ga_ref[0, rows, :].astype(f32)), (ob_ref, g_b))):
            half_g = 0.5 * g
            silu = half_g + half_g * jnp.tanh(half_g)
            for gi in range(N_GROUPS):
                lo = br * D_BRANCH + gi * LANES
                og_ref[rows, lo:lo + LANES] = (src[0, gi, rows, :].astype(f32)
                                               * silu[:, gi * LANES:(gi + 1) * LANES]).astype(bf16)

    def out_stage(rows):
        y = jnp.dot(og_ref[rows, :], w_ref[...], preferred_element_type=f32)
        z = DEEPNORM_ALPHA * x_ref[0, rows, :] + gate * y
        mu = jnp.mean(z, axis=-1, keepdims=True)
        zc = z - mu
        var = jnp.mean(zc * zc, axis=-1, keepdims=True)
        o_ref[0, rows, :] = zc * lax.rsqrt(var + LN_EPS) * lng_ref[...] + lnb_ref[...]

    subs = [slice(r, r + OUT_SUB_ROWS) for r in range(0, tm, OUT_SUB_ROWS)]
    for idx in range(len(subs) + 1):
        if idx < len(subs):
            gate_stage(subs[idx])
        if idx >= 1:
            out_stage(subs[idx - 1])


def _out_projection(o_a, o_b, g_a, x, mod3, w_in_bf, w_out_bf, ln_g, ln_b):
    B, S, D = x.shape
    tm = OUT_ROWS
    att_spec = pl.BlockSpec((1, N_GROUPS, tm, LANES), lambda b, s: (b, 0, s, 0))
    row_spec = pl.BlockSpec((1, tm, D), lambda b, s: (b, s, 0))
    vec_spec = pl.BlockSpec((1, D), lambda b, s: (0, 0))
    return pl.pallas_call(
        _out_kernel,
        grid=(B, S // tm),
        in_specs=[att_spec, att_spec,
                  pl.BlockSpec((1, tm, D_BRANCH), lambda b, s: (b, s, 0)),
                  row_spec,
                  pl.BlockSpec((1, 3, D), lambda b, s: (b, 0, 0)),
                  pl.BlockSpec((D, D_BRANCH), lambda b, s: (0, 7)),
                  pl.BlockSpec(w_out_bf.shape, lambda b, s: (0, 0)),
                  vec_spec, vec_spec],
        out_specs=row_spec,
        out_shape=jax.ShapeDtypeStruct((B, S, D), f32),
        scratch_shapes=[pltpu.VMEM((tm, 2 * D_BRANCH), bf16)],
        compiler_params=pltpu.CompilerParams(
            dimension_semantics=("arbitrary", "arbitrary"), vmem_limit_bytes=VMEM_LIMIT),
        name="out_proj_ln",
    )(o_a, o_b, g_a, x, mod3, w_in_bf, w_out_bf, ln_g.reshape(1, D), ln_b.reshape(1, D))


def _rope_tables(S):
    half = HEAD_DIM // 2
    inv = ROPE_THETA ** (-jnp.arange(half, dtype=f32) / half)
    ang = jnp.arange(S, dtype=jnp.int32).astype(f32)[:, None] * inv[None, :]
    cos, sin = jnp.cos(ang), jnp.sin(ang)
    cos_head = jnp.concatenate([cos, cos], axis=-1)
    sin_head = jnp.concatenate([-sin, sin], axis=-1)
    return (jnp.tile(cos_head, (1, HEADS_PER_GROUP)), jnp.tile(sin_head, (1, HEADS_PER_GROUP)))


def kernel(x, c, w_in, w_out, w_ada, b_ada, ln_g, ln_b):
    B, S, D = x.shape
    cos_t, sin_t = _rope_tables(S)
    for layer in range(w_in.shape[0]):
        mod3 = _modulation(c, w_ada[layer], b_ada[layer]).reshape(B, 3, D)
        w_in_bf = w_in[layer].astype(bf16)
        qa, ka, va, g_a, qb, kb, vb = _projection(x, mod3, cos_t, sin_t, w_in_bf)
        o_a, o_b = _mixers(qa, ka, va, qb, kb, vb)
        x = _out_projection(o_a, o_b, g_a, x, mod3, w_in_bf, w_out[layer].astype(bf16),
                            ln_g[layer], ln_b[layer])
    return x
```

```python
import functools

import numpy as np
import jax
import jax.numpy as jnp
from jax import lax
from jax.experimental import pallas as pl
from jax.experimental.pallas import tpu as pltpu

HEAD_DIM = 64
N_HEADS = 8
D_BRANCH = N_HEADS * HEAD_DIM
DILATED_PATTERNS = ((128, 1), (512, 4), (2048, 16))
MOBA_BLOCK = 256
MOBA_TOPK = 3
ROPE_THETA = 10000.0
LN_EPS = 1e-5
NEG_INF = -1e30
DEPTH = 1
DEEPNORM_ALPHA = (2.0 * DEPTH) ** 0.25
LOG2E = 1.4426950408889634
Q_SCALE = HEAD_DIM ** -0.5 * LOG2E

LANES = 128
HEADS_PER_GROUP = LANES // HEAD_DIM
N_GROUPS = N_HEADS // HEADS_PER_GROUP
ATT_TILE = 256
N_SLOTS = 5
SCORE_CHUNK = 2
SLAB_PERIOD = 16
STAGE_ORDER = (("S", 0), ("E", 4), ("V", 8))
PROJ_ROWS = 512
OUT_ROWS = 1024
OUT_SUB_ROWS = 256
VMEM_LIMIT = 48 * 1024 * 1024

NT_DIMS = (((1,), (1,)), ((), ()))

f32 = jnp.float32
bf16 = jnp.bfloat16


def _split_bf16(a):
    hi = a.astype(bf16)
    return hi, (a - hi.astype(f32)).astype(bf16)


def _mod_kernel(c_ref, w_ref, b_ref, o_ref):
    c_hi, c_lo = _split_bf16(c_ref[...])
    w_hi, w_lo = _split_bf16(w_ref[...])
    dot = lambda a, b: jnp.dot(a, b, preferred_element_type=f32)
    o_ref[...] = dot(c_hi, w_hi) + (dot(c_hi, w_lo) + dot(c_lo, w_hi)) + b_ref[...]


def _modulation(c, w_ada, b_ada):
    B, D = c.shape
    N = w_ada.shape[1]
    bn = 512
    return pl.pallas_call(
        _mod_kernel,
        grid=(N // bn,),
        in_specs=[pl.BlockSpec((B, D), lambda n: (0, 0)),
                  pl.BlockSpec((D, bn), lambda n: (0, n)),
                  pl.BlockSpec((1, bn), lambda n: (0, n))],
        out_specs=pl.BlockSpec((B, bn), lambda n: (0, n)),
        out_shape=jax.ShapeDtypeStruct((B, N), f32),
        name="adaln_mod",
    )(c, w_ada, b_ada.reshape(1, N))


def _modulated(x_ref, mod_ref):
    shift = mod_ref[0, 0:1, :]
    scale = mod_ref[0, 1:2, :]
    return (x_ref[0] * (1.0 + scale) + shift).astype(bf16)


def _proj_kernel(x_ref, mod_ref, cos_ref, sin_ref, w_ref,
                 qa_ref, ka_ref, va_ref, ga_ref, qb_ref, kb_ref, vb_ref):
    tm = x_ref.shape[1]
    h = _modulated(x_ref, mod_ref)
    cos = cos_ref[...]
    sin = sin_ref[...]
    lane = lax.broadcasted_iota(jnp.int32, (tm, LANES), 1)
    first_half = (lane & (HEAD_DIM // 2)) == 0

    def rope(t):
        partner = jnp.where(first_half,
                            pltpu.roll(t, LANES - HEAD_DIM // 2, 1),
                            pltpu.roll(t, HEAD_DIM // 2, 1))
        return t * cos + partner * sin

    targets = (qa_ref, ka_ref, va_ref, ga_ref, qb_ref, kb_ref, vb_ref)
    kinds = ("q", "k", "v", "g", "q", "k", "v")
    for c in range(len(targets)):
        acc = jnp.dot(h, w_ref[:, c * D_BRANCH:(c + 1) * D_BRANCH],
                      preferred_element_type=f32)
        if kinds[c] == "g":
            ga_ref[0] = acc.astype(bf16)
            continue
        for gi in range(N_GROUPS):
            t = acc[:, gi * LANES:(gi + 1) * LANES]
            if kinds[c] == "q":
                t = rope(t) * Q_SCALE
            elif kinds[c] == "k":
                t = rope(t)
            targets[c][0, gi] = t.astype(bf16)


def _projection(x, mod3, cos_t, sin_t, w_in_bf):
    B, S, D = x.shape
    tm = PROJ_ROWS
    n_s = S // tm
    qkv_shape = jax.ShapeDtypeStruct((B, N_GROUPS, S, LANES), bf16)
    qkv_spec = pl.BlockSpec((1, N_GROUPS, tm, LANES), lambda b, s: (b, 0, s, 0))
    g_shape = jax.ShapeDtypeStruct((B, S, D_BRANCH), bf16)
    g_spec = pl.BlockSpec((1, tm, D_BRANCH), lambda b, s: (b, s, 0))
    return pl.pallas_call(
        _proj_kernel,
        grid=(B, n_s),
        in_specs=[pl.BlockSpec((1, tm, D), lambda b, s: (b, s, 0)),
                  pl.BlockSpec((1, 3, D), lambda b, s: (b, 0, 0)),
                  pl.BlockSpec((tm, LANES), lambda b, s: (s, 0)),
                  pl.BlockSpec((tm, LANES), lambda b, s: (s, 0)),
                  pl.BlockSpec((D, 7 * D_BRANCH), lambda b, s: (0, 0))],
        out_specs=[qkv_spec] * 3 + [g_spec] + [qkv_spec] * 3,
        out_shape=[qkv_shape] * 3 + [g_shape] + [qkv_shape] * 3,
        compiler_params=pltpu.CompilerParams(
            dimension_semantics=("arbitrary", "arbitrary"), vmem_limit_bytes=VMEM_LIMIT),
        name="in_proj_rope",
    )(x, mod3, cos_t, sin_t, w_in_bf)


def _own_lanes(h, shape):
    lane = lax.broadcasted_iota(jnp.int32, shape, len(shape) - 1)
    return (lane < HEAD_DIM) if h == 0 else (lane >= HEAD_DIM)


def _own_rows(h, shape):
    row = lax.broadcasted_iota(jnp.int32, shape, 0)
    return (row < HEAD_DIM) if h == 0 else (row >= HEAD_DIM)


def _fill_v_aug_t(v_ref, vaugt_ref):
    vt = v_ref[0, 0].astype(f32).T
    for h in range(HEADS_PER_GROUP):
        vaugt_ref[h] = jnp.where(_own_rows(h, vt.shape), vt, 1.0).astype(bf16)


class _Rows:
    def __init__(self, i, h, key_bias, query_bias, q_ref, k_ref, vaugt_ref, o_ref):
        self.i, self.h = i, h
        self.key_bias, self.query_bias = key_bias, query_bias
        self.q_ref, self.k_ref, self.vaugt_ref, self.o_ref = q_ref, k_ref, vaugt_ref, o_ref
        self.m = self.acc = None


def _score_pass(u, s_ref, slot):
    T = ATT_TILE
    q = u.q_ref[0, 0, u.i * T:(u.i + 1) * T, :]
    qh = jnp.where(_own_lanes(u.h, q.shape), q, jnp.zeros_like(q))
    mrun = None
    for j0 in range(0, u.i + 1, SCORE_CHUNK):
        js = range(j0, min(j0 + SCORE_CHUNK, u.i + 1))
        s = lax.dot_general(u.k_ref[0, 0, js[0] * T:(js[-1] + 1) * T, :], qh, NT_DIMS,
                            preferred_element_type=f32)
        for j in js:
            sj = s[(j - j0) * T:(j - j0 + 1) * T]
            s_ref[slot, j * T:(j + 1) * T, :] = sj
            kb = u.key_bias(j)
            if kb is not None:
                sj = sj + kb
            t = jnp.max(sj.reshape(T // 8, 8, T), axis=0)
            qb = u.query_bias(j)
            if qb is not None:
                t = t + qb
            mrun = t if mrun is None else jnp.maximum(mrun, t)
    u.m = jnp.max(mrun, axis=0, keepdims=True)


def _exp_pass(u, s_ref, p_ref, slot):
    T = ATT_TILE
    for j in range(u.i + 1):
        m = u.m
        qb = u.query_bias(j)
        if qb is not None:
            m = m - qb
        rows = slice(j * T, (j + 1) * T)
        sj = s_ref[slot, rows, :]
        kb = u.key_bias(j)
        if kb is not None:
            sj = sj + kb
        p_ref[slot, rows, :] = jnp.exp2(sj - m).astype(bf16)


def _value_pass(u, p_ref, slot):
    n = (u.i + 1) * ATT_TILE
    u.acc = jnp.dot(u.vaugt_ref[u.h, :, 0:n], p_ref[slot, 0:n, :],
                    preferred_element_type=f32)


def _run_rows(units, s_ref, p_ref):
    T = ATT_TILE
    done = {}
    units = list(units)

    def score(idx):
        if 0 <= idx < len(units):
            _score_pass(units[idx], s_ref, idx % N_SLOTS)

    def exp(idx):
        if 0 <= idx < len(units):
            _exp_pass(units[idx], s_ref, p_ref, idx % N_SLOTS)

    def value(idx):
        if 0 <= idx < len(units):
            prev = units[idx]
            _value_pass(prev, p_ref, idx % N_SLOTS)
            den_row = HEAD_DIM if prev.h == 0 else 0
            heads = done.setdefault((id(prev.o_ref), prev.i), {})
            heads[prev.h] = prev.acc / prev.acc[den_row:den_row + 1, :]
            if len(heads) == HEADS_PER_GROUP:
                out_t = jnp.where(_own_rows(0, heads[0].shape), heads[0], heads[1])
                prev.o_ref[0, 0, prev.i * T:(prev.i + 1) * T, :] = out_t.T.astype(prev.o_ref.dtype)
                del done[(id(prev.o_ref), prev.i)]

    stages = {"S": score, "E": exp, "V": value}
    for idx in range(len(units) + max(lag for _, lag in STAGE_ORDER)):
        for name, lag in STAGE_ORDER:
            stages[name](idx - lag)


def _dilated_units(periodic_offsets, q_ref, k_ref, slab_ref, o_ref, vaugt_ref):
    T = ATT_TILE
    n_t = q_ref.shape[2] // T

    def key_bias(i, j):
        row0 = (n_t - 1 - i + j) * T
        if i - j in periodic_offsets:
            return jnp.tile(slab_ref[row0:row0 + SLAB_PERIOD, :], (T // SLAB_PERIOD, 1))
        return slab_ref[row0:row0 + T, :]

    return [_Rows(i, h, lambda j, i=i: key_bias(i, j), lambda j: None,
                  q_ref, k_ref, vaugt_ref, o_ref)
            for i in range(n_t) for h in range(HEADS_PER_GROUP)]


def _moba_units(q_ref, k_ref, o_ref, vaugt_ref, kmean_ref):
    T = ATT_TILE
    S = q_ref.shape[2]
    n_t = S // T

    kmean_ref[...] = jnp.zeros_like(kmean_ref)
    for j in range(n_t):
        kj = k_ref[0, 0, j * T:(j + 1) * T, :]
        kmean_ref[j:j + 1, :] = jnp.sum(kj.astype(f32), axis=0, keepdims=True) * (1.0 / T)
    kmean = kmean_ref[...]
    km_parts = []
    for h in range(HEADS_PER_GROUP):
        km_parts += list(_split_bf16(jnp.where(_own_lanes(h, kmean.shape), kmean, 0.0)))
    km_rows = kmean.shape[0]
    c0 = min((MOBA_TOPK + 1) * T, S - T)
    W = S - c0
    gates = lax.dot_general(jnp.concatenate(km_parts, axis=0), q_ref[0, 0, c0:S, :], NT_DIMS,
                            preferred_element_type=f32)

    key_i = lax.broadcasted_iota(jnp.int32, (T, T), 0)
    qry_i = lax.broadcasted_iota(jnp.int32, (T, T), 1)
    causal_bias = jnp.where(key_i <= qry_i, 0.0, NEG_INF).astype(f32)

    blk = lax.broadcasted_iota(jnp.int32, (8, W), 0)
    own_blk = (lax.broadcasted_iota(jnp.int32, (8, W), 1) + c0) // T
    past = blk < own_blk
    drop = []
    for h in range(HEADS_PER_GROUP):
        hi0 = 2 * h * km_rows
        gate = gates[hi0:hi0 + 8] + gates[hi0 + km_rows:hi0 + km_rows + 8]
        cnt = jnp.zeros((8, W), f32)
        for jp in range(n_t - 1):
            gj = gate[jp:jp + 1, :]
            ahead = (gj > gate) | ((gj == gate) & (jp < blk))
            cnt = cnt + jnp.where(ahead & (jp < own_blk), 1.0, 0.0)
        drop.append(jnp.where(past & (cnt >= float(MOBA_TOPK)), NEG_INF, 0.0))

    def query_bias(i, h, j):
        if j == i or i * T < c0:
            return None
        return drop[h][j:j + 1, i * T - c0:(i + 1) * T - c0]

    return [_Rows(i, h,
                  lambda j, i=i: causal_bias if j == i else None,
                  lambda j, i=i, h=h: query_bias(i, h, j),
                  q_ref, k_ref, vaugt_ref, o_ref)
            for i in range(n_t) for h in range(HEADS_PER_GROUP)]


def _mixers_kernel(periodic_offsets, qa_ref, ka_ref, va_ref, qb_ref, kb_ref, vb_ref, slab_ref,
                   oa_ref, ob_ref, s_ref, p_ref, vaugt_a_ref, vaugt_b_ref, kmean_ref):
    _fill_v_aug_t(va_ref, vaugt_a_ref)
    _fill_v_aug_t(vb_ref, vaugt_b_ref)
    dilated = _dilated_units(periodic_offsets, qa_ref, ka_ref, slab_ref, oa_ref, vaugt_a_ref)
    moba = _moba_units(qb_ref, kb_ref, ob_ref, vaugt_b_ref, kmean_ref)
    _run_rows([u for pair in zip(dilated, moba) for u in pair], s_ref, p_ref)


def _dilated_bias_slab(n_t):
    T = ATT_TILE
    key = np.arange(T)[:, None]
    qry = np.arange(T)[None, :]
    slab = np.empty((n_t * T, T), np.float32)
    for b in range(n_t):
        dist = (n_t - 1 - b) * T + qry - key
        mult = np.zeros((T, T), np.float64)
        for window, dil in DILATED_PATTERNS:
            mult += (dist >= 0) & (dist <= window) & (dist % dil == 0)
        slab[b * T:(b + 1) * T, :] = np.where(mult > 0, np.log2(np.maximum(mult, 1.0)), NEG_INF)
    periodic = frozenset(
        n_t - 1 - b for b in range(n_t)
        if np.array_equal(slab[b * T:(b + 1) * T],
                          np.tile(slab[b * T:b * T + SLAB_PERIOD], (T // SLAB_PERIOD, 1))))
    return jnp.asarray(slab), periodic


def _mixers(qa, ka, va, qb, kb, vb):
    B, G, S, _ = qa.shape
    assert ATT_TILE == MOBA_BLOCK and S // ATT_TILE <= 8
    slab, periodic = _dilated_bias_slab(S // ATT_TILE)
    blk = pl.BlockSpec((1, 1, S, LANES), lambda b, g: (b, g, 0, 0))
    vaugt = pltpu.VMEM((HEADS_PER_GROUP, LANES, S), bf16)
    out = jax.ShapeDtypeStruct(qa.shape, bf16)
    return pl.pallas_call(
        functools.partial(_mixers_kernel, periodic),
        grid=(B, G),
        in_specs=[blk] * 6 + [pl.BlockSpec(slab.shape, lambda b, g: (0, 0))],
        out_specs=[blk, blk],
        out_shape=[out, out],
        scratch_shapes=[pltpu.VMEM((N_SLOTS, S, ATT_TILE), f32),
                        pltpu.VMEM((N_SLOTS, S, ATT_TILE), bf16),
                        vaugt, vaugt,
                        pltpu.VMEM((16, LANES), f32)],
        compiler_params=pltpu.CompilerParams(
            dimension_semantics=("arbitrary", "arbitrary"), vmem_limit_bytes=VMEM_LIMIT),
        name="mixers_attn",
    )(qa, ka, va, qb, kb, vb, slab)


def _out_kernel(oa_ref, ob_ref, ga_ref, x_ref, mod_ref, wgb_ref, w_ref, lng_ref, lnb_ref,
                o_ref, og_ref):
    tm = x_ref.shape[1]
    shift, scale, gate = mod_ref[0, 0:1, :], mod_ref[0, 1:2, :], mod_ref[0, 2:3, :]

    def gate_stage(rows):
        h = (x_ref[0, rows, :] * (1.0 + scale) + shift).astype(bf16)
        g_b = jnp.dot(h, wgb_ref[...], preferred_element_type=f32)
        for br, (src, g) in enumerate(((oa_ref, ga_ref[0, rows, :].astype(f32)), (ob_ref, g_b))):
            half_g = 0.5 * g
            silu = half_g + half_g * jnp.tanh(half_g)
            for gi in range(N_GROUPS):
                lo = br * D_BRANCH + gi * LANES
                og_ref[rows, lo:lo + LANES] = (src[0, gi, rows, :].astype(f32)
                                               * silu[:, gi * LANES:(gi + 1) * LANES]).astype(bf16)

    def out_stage(rows):
        y = jnp.dot(og_ref[rows, :], w_ref[...], preferred_element_type=f32)
        z = DEEPNORM_ALPHA * x_ref[0, rows, :] + gate * y
        mu = jnp.mean(z, axis=-1, keepdims=True)
        zc = z - mu
        var = jnp.mean(zc * zc, axis=-1, keepdims=True)
        o_ref[0, rows, :] = zc * lax.rsqrt(var + LN_EPS) * lng_ref[...] + lnb_ref[...]

    subs = [slice(r, r + OUT_SUB_ROWS) for r in range(0, tm, OUT_SUB_ROWS)]
    for idx in range(len(subs) + 1):
        if idx < len(subs):
            gate_stage(subs[idx])
        if idx >= 1:
            out_stage(subs[idx - 1])


def _out_projection(o_a, o_b, g_a, x, mod3, w_in_bf, w_out_bf, ln_g, ln_b):
    B, S, D = x.shape
    tm = OUT_ROWS
    att_spec = pl.BlockSpec((1, N_GROUPS, tm, LANES), lambda b, s: (b, 0, s, 0))
    row_spec = pl.BlockSpec((1, tm, D), lambda b, s: (b, s, 0))
    vec_spec = pl.BlockSpec((1, D), lambda b, s: (0, 0))
    return pl.pallas_call(
        _out_kernel,
        grid=(B, S // tm),
        in_specs=[att_spec, att_spec,
                  pl.BlockSpec((1, tm, D_BRANCH), lambda b, s: (b, s, 0)),
                  row_spec,
                  pl.BlockSpec((1, 3, D), lambda b, s: (b, 0, 0)),
                  pl.BlockSpec((D, D_BRANCH), lambda b, s: (0, 7)),
                  pl.BlockSpec(w_out_bf.shape, lambda b, s: (0, 0)),
                  vec_spec, vec_spec],
        out_specs=row_spec,
        out_shape=jax.ShapeDtypeStruct((B, S, D), f32),
        scratch_shapes=[pltpu.VMEM((tm, 2 * D_BRANCH), bf16)],
        compiler_params=pltpu.CompilerParams(
            dimension_semantics=("arbitrary", "arbitrary"), vmem_limit_bytes=VMEM_LIMIT),
        name="out_proj_ln",
    )(o_a, o_b, g_a, x, mod3, w_in_bf, w_out_bf, ln_g.reshape(1, D), ln_b.reshape(1, D))


def _rope_tables(S):
    half = HEAD_DIM // 2
    inv = ROPE_THETA ** (-jnp.arange(half, dtype=f32) / half)
    ang = jnp.arange(S, dtype=jnp.int32).astype(f32)[:, None] * inv[None, :]
    cos, sin = jnp.cos(ang), jnp.sin(ang)
    cos_head = jnp.concatenate([cos, cos], axis=-1)
    sin_head = jnp.concatenate([-sin, sin], axis=-1)
    return (jnp.tile(cos_head, (1, HEADS_PER_GROUP)), jnp.tile(sin_head, (1, HEADS_PER_GROUP)))


def kernel(x, c, w_in, w_out, w_ada, b_ada, ln_g, ln_b):
    B, S, D = x.shape
    cos_t, sin_t = _rope_tables(S)
    for layer in range(w_in.shape[0]):
        mod3 = _modulation(c, w_ada[layer], b_ada[layer]).reshape(B, 3, D)
        w_in_bf = w_in[layer].astype(bf16)
        qa, ka, va, g_a, qb, kb, vb = _projection(x, mod3, cos_t, sin_t, w_in_bf)
        o_a, o_b = _mixers(qa, ka, va, qb, kb, vb)
        x = _out_projection(o_a, o_b, g_a, x, mod3, w_in_bf, w_out[layer].astype(bf16),
                            ln_g[layer], ln_b[layer])
    return x
```

```python
import functools

import numpy as np
import jax
import jax.numpy as jnp
from jax import lax
from jax.experimental import pallas as pl
from jax.experimental.pallas import tpu as pltpu

HEAD_DIM = 64
N_HEADS = 8
D_BRANCH = N_HEADS * HEAD_DIM
DILATED_PATTERNS = ((128, 1), (512, 4), (2048, 16))
MOBA_BLOCK = 256
MOBA_TOPK = 3
ROPE_THETA = 10000.0
LN_EPS = 1e-5
NEG_INF = -1e30
DEPTH = 1
DEEPNORM_ALPHA = (2.0 * DEPTH) ** 0.25
LOG2E = 1.4426950408889634
Q_SCALE = HEAD_DIM ** -0.5 * LOG2E

LANES = 128
HEADS_PER_GROUP = LANES // HEAD_DIM
N_GROUPS = N_HEADS // HEADS_PER_GROUP
ATT_TILE = 256
N_SLOTS = 5
SCORE_CHUNK = 2
ONLINE_WINDOW = 3
SLAB_PERIOD = 16
STAGE_ORDER = (("S", 0), ("E", 4), ("V", 8))
PROJ_ROWS = 512
OUT_ROWS = 1024
OUT_SUB_ROWS = 256
VMEM_LIMIT = 48 * 1024 * 1024

NT_DIMS = (((1,), (1,)), ((), ()))

f32 = jnp.float32
bf16 = jnp.bfloat16


def _split_bf16(a):
    hi = a.astype(bf16)
    return hi, (a - hi.astype(f32)).astype(bf16)


def _mod_kernel(c_ref, w_ref, b_ref, o_ref):
    c_hi, c_lo = _split_bf16(c_ref[...])
    w_hi, w_lo = _split_bf16(w_ref[...])
    dot = lambda a, b: jnp.dot(a, b, preferred_element_type=f32)
    o_ref[...] = dot(c_hi, w_hi) + (dot(c_hi, w_lo) + dot(c_lo, w_hi)) + b_ref[...]


def _modulation(c, w_ada, b_ada):
    B, D = c.shape
    N = w_ada.shape[1]
    bn = 512
    return pl.pallas_call(
        _mod_kernel,
        grid=(N // bn,),
        in_specs=[pl.BlockSpec((B, D), lambda n: (0, 0)),
                  pl.BlockSpec((D, bn), lambda n: (0, n)),
                  pl.BlockSpec((1, bn), lambda n: (0, n))],
        out_specs=pl.BlockSpec((B, bn), lambda n: (0, n)),
        out_shape=jax.ShapeDtypeStruct((B, N), f32),
        name="adaln_mod",
    )(c, w_ada, b_ada.reshape(1, N))


def _modulated(x_ref, mod_ref):
    shift = mod_ref[0, 0:1, :]
    scale = mod_ref[0, 1:2, :]
    return (x_ref[0] * (1.0 + scale) + shift).astype(bf16)


def _proj_kernel(x_ref, mod_ref, cos_ref, sin_ref, w_ref,
                 qa_ref, ka_ref, va_ref, ga_ref, qb_ref, kb_ref, vb_ref):
    tm = x_ref.shape[1]
    h = _modulated(x_ref, mod_ref)
    cos = cos_ref[...]
    sin = sin_ref[...]
    lane = lax.broadcasted_iota(jnp.int32, (tm, LANES), 1)
    first_half = (lane & (HEAD_DIM // 2)) == 0

    def rope(t):
        partner = jnp.where(first_half,
                            pltpu.roll(t, LANES - HEAD_DIM // 2, 1),
                            pltpu.roll(t, HEAD_DIM // 2, 1))
        return t * cos + partner * sin

    targets = (qa_ref, ka_ref, va_ref, ga_ref, qb_ref, kb_ref, vb_ref)
    kinds = ("q", "k", "v", "g", "q", "k", "v")
    for c in range(len(targets)):
        acc = jnp.dot(h, w_ref[:, c * D_BRANCH:(c + 1) * D_BRANCH],
                      preferred_element_type=f32)
        if kinds[c] == "g":
            ga_ref[0] = acc.astype(bf16)
            continue
        for gi in range(N_GROUPS):
            t = acc[:, gi * LANES:(gi + 1) * LANES]
            if kinds[c] == "q":
                t = rope(t) * Q_SCALE
            elif kinds[c] == "k":
                t = rope(t)
            targets[c][0, gi] = t.astype(bf16)


def _projection(x, mod3, cos_t, sin_t, w_in_bf):
    B, S, D = x.shape
    tm = PROJ_ROWS
    n_s = S // tm
    qkv_shape = jax.ShapeDtypeStruct((B, N_GROUPS, S, LANES), bf16)
    qkv_spec = pl.BlockSpec((1, N_GROUPS, tm, LANES), lambda b, s: (b, 0, s, 0))
    g_shape = jax.ShapeDtypeStruct((B, S, D_BRANCH), bf16)
    g_spec = pl.BlockSpec((1, tm, D_BRANCH), lambda b, s: (b, s, 0))
    return pl.pallas_call(
        _proj_kernel,
        grid=(B, n_s),
        in_specs=[pl.BlockSpec((1, tm, D), lambda b, s: (b, s, 0)),
                  pl.BlockSpec((1, 3, D), lambda b, s: (b, 0, 0)),
                  pl.BlockSpec((tm, LANES), lambda b, s: (s, 0)),
                  pl.BlockSpec((tm, LANES), lambda b, s: (s, 0)),
                  pl.BlockSpec((D, 7 * D_BRANCH), lambda b, s: (0, 0))],
        out_specs=[qkv_spec] * 3 + [g_spec] + [qkv_spec] * 3,
        out_shape=[qkv_shape] * 3 + [g_shape] + [qkv_shape] * 3,
        compiler_params=pltpu.CompilerParams(
            dimension_semantics=("arbitrary", "arbitrary"), vmem_limit_bytes=VMEM_LIMIT),
        name="in_proj_rope",
    )(x, mod3, cos_t, sin_t, w_in_bf)


def _own_lanes(h, shape):
    lane = lax.broadcasted_iota(jnp.int32, shape, len(shape) - 1)
    return (lane < HEAD_DIM) if h == 0 else (lane >= HEAD_DIM)


def _own_rows(h, shape):
    row = lax.broadcasted_iota(jnp.int32, shape, 0)
    return (row < HEAD_DIM) if h == 0 else (row >= HEAD_DIM)


def _fill_v_aug_t(v_ref, vaugt_ref):
    vt = v_ref[0, 0].astype(f32).T
    for h in range(HEADS_PER_GROUP):
        vaugt_ref[h] = jnp.where(_own_rows(h, vt.shape), vt, 1.0).astype(bf16)


class _Rows:
    def __init__(self, i, h, key_bias, query_bias, q_ref, k_ref, vaugt_ref, o_ref):
        self.i, self.h = i, h
        self.key_bias, self.query_bias = key_bias, query_bias
        self.q_ref, self.k_ref, self.vaugt_ref, self.o_ref = q_ref, k_ref, vaugt_ref, o_ref
        self.m = self.acc = None


def _score_pass(u, s_ref, slot):
    T = ATT_TILE
    q = u.q_ref[0, 0, u.i * T:(u.i + 1) * T, :]
    qh = jnp.where(_own_lanes(u.h, q.shape), q, jnp.zeros_like(q))
    mrun = None
    for j0 in range(0, u.i + 1, SCORE_CHUNK):
        js = range(j0, min(j0 + SCORE_CHUNK, u.i + 1))
        s = lax.dot_general(u.k_ref[0, 0, js[0] * T:(js[-1] + 1) * T, :], qh, NT_DIMS,
                            preferred_element_type=f32)
        for j in js:
            sj = s[(j - j0) * T:(j - j0 + 1) * T]
            kb = u.key_bias(j)
            if kb is not None:
                sj = sj + kb
            s_ref[slot, j * T:(j + 1) * T, :] = sj
            t = jnp.max(sj.reshape(T // 8, 8, T), axis=0)
            qb = u.query_bias(j)
            if qb is not None:
                t = t + qb
            mrun = t if mrun is None else jnp.maximum(mrun, t)
    u.m = jnp.max(mrun, axis=0, keepdims=True)


def _exp_pass(u, s_ref, p_ref, slot):
    T = ATT_TILE
    for j in range(u.i + 1):
        m = u.m
        qb = u.query_bias(j)
        if qb is not None:
            m = m - qb
        rows = slice(j * T, (j + 1) * T)
        p_ref[slot, rows, :] = jnp.exp2(s_ref[slot, rows, :] - m).astype(bf16)


def _value_pass(u, p_ref, slot):
    n = (u.i + 1) * ATT_TILE
    u.acc = jnp.dot(u.vaugt_ref[u.h, :, 0:n], p_ref[slot, 0:n, :],
                    preferred_element_type=f32)


def _run_rows(units, s_ref, p_ref):
    T = ATT_TILE
    done = {}
    units = list(units)

    def score(idx):
        if 0 <= idx < len(units):
            _score_pass(units[idx], s_ref, idx % N_SLOTS)

    def exp(idx):
        if 0 <= idx < len(units):
            _exp_pass(units[idx], s_ref, p_ref, idx % N_SLOTS)

    def value(idx):
        if 0 <= idx < len(units):
            prev = units[idx]
            _value_pass(prev, p_ref, idx % N_SLOTS)
            den_row = HEAD_DIM if prev.h == 0 else 0
            heads = done.setdefault((id(prev.o_ref), prev.i), {})
            heads[prev.h] = prev.acc / prev.acc[den_row:den_row + 1, :]
            if len(heads) == HEADS_PER_GROUP:
                out_t = jnp.where(_own_rows(0, heads[0].shape), heads[0], heads[1])
                prev.o_ref[0, 0, prev.i * T:(prev.i + 1) * T, :] = out_t.T.astype(prev.o_ref.dtype)
                del done[(id(prev.o_ref), prev.i)]

    stages = {"S": score, "E": exp, "V": value}
    for idx in range(len(units) + max(lag for _, lag in STAGE_ORDER)):
        for name, lag in STAGE_ORDER:
            stages[name](idx - lag)


def _online_tile(u, j):
    T = ATT_TILE
    if u.qh is None:
        q = u.q_ref[0, 0, u.i * T:(u.i + 1) * T, :]
        u.qh = jnp.where(_own_lanes(u.h, q.shape), q, jnp.zeros_like(q))
    s = lax.dot_general(u.k_ref[0, 0, j * T:(j + 1) * T, :], u.qh, NT_DIMS,
                        preferred_element_type=f32)
    kb = u.key_bias(j)
    if kb is not None:
        s = s + kb
    t = jnp.max(s.reshape(T // 8, 8, T), axis=0)
    qb = u.query_bias(j)
    if qb is not None:
        t = t + qb
    tmax = jnp.max(t, axis=0, keepdims=True)
    m_new = tmax if u.m is None else jnp.maximum(u.m, tmax)
    p = jnp.exp2(s - (m_new if qb is None else m_new - qb)).astype(bf16)
    alpha = None if u.m is None else jnp.exp2(u.m - m_new)
    u.m = m_new
    u.queue.append((j, p, alpha))


def _online_value(u):
    T = ATT_TILE
    j, p, alpha = u.queue.pop(0)
    pv = jnp.dot(u.vaugt_ref[u.h, :, j * T:(j + 1) * T], p, preferred_element_type=f32)
    u.acc = pv if alpha is None else u.acc * alpha + pv


def _run_rows_online(units):
    T = ATT_TILE
    done = {}
    pending = list(units)
    active = []

    def finish(u):
        den_row = HEAD_DIM if u.h == 0 else 0
        heads = done.setdefault((id(u.o_ref), u.i), {})
        heads[u.h] = u.acc / u.acc[den_row:den_row + 1, :]
        if len(heads) == HEADS_PER_GROUP:
            out_t = jnp.where(_own_rows(0, heads[0].shape), heads[0], heads[1])
            u.o_ref[0, 0, u.i * T:(u.i + 1) * T, :] = out_t.T.astype(u.o_ref.dtype)
            del done[(id(u.o_ref), u.i)]

    while pending or active:
        while len(active) < ONLINE_WINDOW and pending:
            u = pending.pop(0)
            u.qh, u.queue = None, []
            u.todo = [u.i] + list(range(u.i))
            active.append(u)
        lagging = [u for u in active if u.queue]
        for u in active:
            if u.todo:
                _online_tile(u, u.todo.pop(0))
        for u in lagging:
            _online_value(u)
        for u in list(active):
            if not u.todo and len(u.queue) <= 0:
                active.remove(u)
                finish(u)


def _dilated_units(periodic_offsets, q_ref, k_ref, slab_ref, o_ref, vaugt_ref):
    T = ATT_TILE
    n_t = q_ref.shape[2] // T

    def key_bias(i, j):
        row0 = (n_t - 1 - i + j) * T
        if i - j in periodic_offsets:
            return jnp.tile(slab_ref[row0:row0 + SLAB_PERIOD, :], (T // SLAB_PERIOD, 1))
        return slab_ref[row0:row0 + T, :]

    return [_Rows(i, h, lambda j, i=i: key_bias(i, j), lambda j: None,
                  q_ref, k_ref, vaugt_ref, o_ref)
            for i in range(n_t) for h in range(HEADS_PER_GROUP)]


def _moba_units(q_ref, k_ref, o_ref, vaugt_ref, kmean_ref):
    T = ATT_TILE
    S = q_ref.shape[2]
    n_t = S // T

    kmean_ref[...] = jnp.zeros_like(kmean_ref)
    for j in range(n_t):
        kj = k_ref[0, 0, j * T:(j + 1) * T, :]
        kmean_ref[j:j + 1, :] = jnp.sum(kj.astype(f32), axis=0, keepdims=True) * (1.0 / T)
    kmean = kmean_ref[...]
    km_parts = []
    for h in range(HEADS_PER_GROUP):
        km_parts += list(_split_bf16(jnp.where(_own_lanes(h, kmean.shape), kmean, 0.0)))
    km_rows = kmean.shape[0]
    c0 = min((MOBA_TOPK + 1) * T, S - T)
    W = S - c0
    gates = lax.dot_general(jnp.concatenate(km_parts, axis=0), q_ref[0, 0, c0:S, :], NT_DIMS,
                            preferred_element_type=f32)

    key_i = lax.broadcasted_iota(jnp.int32, (T, T), 0)
    qry_i = lax.broadcasted_iota(jnp.int32, (T, T), 1)
    causal_bias = jnp.where(key_i <= qry_i, 0.0, NEG_INF).astype(f32)

    blk = lax.broadcasted_iota(jnp.int32, (8, W), 0)
    own_blk = (lax.broadcasted_iota(jnp.int32, (8, W), 1) + c0) // T
    past = blk < own_blk
    drop = []
    for h in range(HEADS_PER_GROUP):
        hi0 = 2 * h * km_rows
        gate = gates[hi0:hi0 + 8] + gates[hi0 + km_rows:hi0 + km_rows + 8]
        cnt = jnp.zeros((8, W), f32)
        for jp in range(n_t - 1):
            gj = gate[jp:jp + 1, :]
            ahead = (gj > gate) | ((gj == gate) & (jp < blk))
            cnt = cnt + jnp.where(ahead & (jp < own_blk), 1.0, 0.0)
        drop.append(jnp.where(past & (cnt >= float(MOBA_TOPK)), NEG_INF, 0.0))

    def query_bias(i, h, j):
        if j == i or i * T < c0:
            return None
        return drop[h][j:j + 1, i * T - c0:(i + 1) * T - c0]

    return [_Rows(i, h,
                  lambda j, i=i: causal_bias if j == i else None,
                  lambda j, i=i, h=h: query_bias(i, h, j),
                  q_ref, k_ref, vaugt_ref, o_ref)
            for i in range(n_t) for h in range(HEADS_PER_GROUP)]


def _mixers_kernel(periodic_offsets, qa_ref, ka_ref, va_ref, qb_ref, kb_ref, vb_ref, slab_ref,
                   oa_ref, ob_ref, s_ref, p_ref, vaugt_a_ref, vaugt_b_ref, kmean_ref):
    _fill_v_aug_t(va_ref, vaugt_a_ref)
    _fill_v_aug_t(vb_ref, vaugt_b_ref)
    dilated = _dilated_units(periodic_offsets, qa_ref, ka_ref, slab_ref, oa_ref, vaugt_a_ref)
    moba = _moba_units(qb_ref, kb_ref, ob_ref, vaugt_b_ref, kmean_ref)
    _run_rows_online([u for pair in zip(dilated, moba) for u in pair])


def _dilated_bias_slab(n_t):
    T = ATT_TILE
    key = np.arange(T)[:, None]
    qry = np.arange(T)[None, :]
    slab = np.empty((n_t * T, T), np.float32)
    for b in range(n_t):
        dist = (n_t - 1 - b) * T + qry - key
        mult = np.zeros((T, T), np.float64)
        for window, dil in DILATED_PATTERNS:
            mult += (dist >= 0) & (dist <= window) & (dist % dil == 0)
        slab[b * T:(b + 1) * T, :] = np.where(mult > 0, np.log2(np.maximum(mult, 1.0)), NEG_INF)
    periodic = frozenset(
        n_t - 1 - b for b in range(n_t)
        if np.array_equal(slab[b * T:(b + 1) * T],
                          np.tile(slab[b * T:b * T + SLAB_PERIOD], (T // SLAB_PERIOD, 1))))
    return jnp.asarray(slab), periodic


def _mixers(qa, ka, va, qb, kb, vb):
    B, G, S, _ = qa.shape
    assert ATT_TILE == MOBA_BLOCK and S // ATT_TILE <= 8
    slab, periodic = _dilated_bias_slab(S // ATT_TILE)
    blk = pl.BlockSpec((1, 1, S, LANES), lambda b, g: (b, g, 0, 0))
    vaugt = pltpu.VMEM((HEADS_PER_GROUP, LANES, S), bf16)
    out = jax.ShapeDtypeStruct(qa.shape, bf16)
    return pl.pallas_call(
        functools.partial(_mixers_kernel, periodic),
        grid=(B, G),
        in_specs=[blk] * 6 + [pl.BlockSpec(slab.shape, lambda b, g: (0, 0))],
        out_specs=[blk, blk],
        out_shape=[out, out],
        scratch_shapes=[pltpu.VMEM((N_SLOTS, S, ATT_TILE), f32),
                        pltpu.VMEM((N_SLOTS, S, ATT_TILE), bf16),
                        vaugt, vaugt,
                        pltpu.VMEM((16, LANES), f32)],
        compiler_params=pltpu.CompilerParams(
            dimension_semantics=("arbitrary", "arbitrary"), vmem_limit_bytes=VMEM_LIMIT),
        name="mixers_attn",
    )(qa, ka, va, qb, kb, vb, slab)


def _out_kernel(oa_ref, ob_ref, ga_ref, x_ref, mod_ref, wgb_ref, w_ref, lng_ref, lnb_ref,
                o_ref, og_ref):
    tm = x_ref.shape[1]
    shift, scale, gate = mod_ref[0, 0:1, :], mod_ref[0, 1:2, :], mod_ref[0, 2:3, :]

    def gate_stage(rows):
        h = (x_ref[0, rows, :] * (1.0 + scale) + shift).astype(bf16)
        g_b = jnp.dot(h, wgb_ref[...], preferred_element_type=f32)
        for br, (src, g) in enumerate(((oa_ref, ga_ref[0, rows, :].astype(f32)), (ob_ref, g_b))):
            half_g = 0.5 * g
            silu = half_g + half_g * jnp.tanh(half_g)
            for gi in range(N_GROUPS):
                lo = br * D_BRANCH + gi * LANES
                og_ref[rows, lo:lo + LANES] = (src[0, gi, rows, :].astype(f32)
                                               * silu[:, gi * LANES:(gi + 1) * LANES]).astype(bf16)

    def out_stage(rows):
        y = jnp.dot(og_ref[rows, :], w_ref[...], preferred_element_type=f32)
        z = DEEPNORM_ALPHA * x_ref[0, rows, :] + gate * y
        mu = jnp.mean(z, axis=-1, keepdims=True)
        zc = z - mu
        var = jnp.mean(zc * zc, axis=-1, keepdims=True)
        o_ref[0, rows, :] = zc * lax.rsqrt(var + LN_EPS) * lng_ref[...] + lnb_ref[...]

    subs = [slice(r, r + OUT_SUB_ROWS) for r in range(0, tm, OUT_SUB_ROWS)]
    for idx in range(len(subs) + 1):
        if idx < len(subs):
            gate_stage(subs[idx])
        if idx >= 1:
            out_stage(subs[idx - 1])


def _out_projection(o_a, o_b, g_a, x, mod3, w_in_bf, w_out_bf, ln_g, ln_b):
    B, S, D = x.shape
    tm = OUT_ROWS
    att_spec = pl.BlockSpec((1, N_GROUPS, tm, LANES), lambda b, s: (b, 0, s, 0))
    row_spec = pl.BlockSpec((1, tm, D), lambda b, s: (b, s, 0))
    vec_spec = pl.BlockSpec((1, D), lambda b, s: (0, 0))
    return pl.pallas_call(
        _out_kernel,
        grid=(B, S // tm),
        in_specs=[att_spec, att_spec,
                  pl.BlockSpec((1, tm, D_BRANCH), lambda b, s: (b, s, 0)),
                  row_spec,
                  pl.BlockSpec((1, 3, D), lambda b, s: (b, 0, 0)),
                  pl.BlockSpec((D, D_BRANCH), lambda b, s: (0, 7)),
                  pl.BlockSpec(w_out_bf.shape, lambda b, s: (0, 0)),
                  vec_spec, vec_spec],
        out_specs=row_spec,
        out_shape=jax.ShapeDtypeStruct((B, S, D), f32),
        scratch_shapes=[pltpu.VMEM((tm, 2 * D_BRANCH), bf16)],
        compiler_params=pltpu.CompilerParams(
            dimension_semantics=("arbitrary", "arbitrary"), vmem_limit_bytes=VMEM_LIMIT),
        name="out_proj_ln",
    )(o_a, o_b, g_a, x, mod3, w_in_bf, w_out_bf, ln_g.reshape(1, D), ln_b.reshape(1, D))


def _rope_tables(S):
    half = HEAD_DIM // 2
    inv = ROPE_THETA ** (-jnp.arange(half, dtype=f32) / half)
    ang = jnp.arange(S, dtype=jnp.int32).astype(f32)[:, None] * inv[None, :]
    cos, sin = jnp.cos(ang), jnp.sin(ang)
    cos_head = jnp.concatenate([cos, cos], axis=-1)
    sin_head = jnp.concatenate([-sin, sin], axis=-1)
    return (jnp.tile(cos_head, (1, HEADS_PER_GROUP)), jnp.tile(sin_head, (1, HEADS_PER_GROUP)))


def kernel(x, c, w_in, w_out, w_ada, b_ada, ln_g, ln_b):
    B, S, D = x.shape
    cos_t, sin_t = _rope_tables(S)
    for layer in range(w_in.shape[0]):
        mod3 = _modulation(c, w_ada[layer], b_ada[layer]).reshape(B, 3, D)
        w_in_bf = w_in[layer].astype(bf16)
        qa, ka, va, g_a, qb, kb, vb = _projection(x, mod3, cos_t, sin_t, w_in_bf)
        o_a, o_b = _mixers(qa, ka, va, qb, kb, vb)
        x = _out_projection(o_a, o_b, g_a, x, mod3, w_in_bf, w_out[layer].astype(bf16),
                            ln_g[layer], ln_b[layer])
    return x
```

```python
import functools

import numpy as np
import jax
import jax.numpy as jnp
from jax import lax
from jax.experimental import pallas as pl
from jax.experimental.pallas import tpu as pltpu

HEAD_DIM = 64
N_HEADS = 8
D_BRANCH = N_HEADS * HEAD_DIM
DILATED_PATTERNS = ((128, 1), (512, 4), (2048, 16))
MOBA_BLOCK = 256
MOBA_TOPK = 3
ROPE_THETA = 10000.0
LN_EPS = 1e-5
NEG_INF = -1e30
DEPTH = 1
DEEPNORM_ALPHA = (2.0 * DEPTH) ** 0.25
LOG2E = 1.4426950408889634
Q_SCALE = HEAD_DIM ** -0.5 * LOG2E

LANES = 128
HEADS_PER_GROUP = LANES // HEAD_DIM
N_GROUPS = N_HEADS // HEADS_PER_GROUP
ATT_TILE = 256
ONLINE_WINDOW = 3
VALUE_LAG = 2
SLAB_PERIOD = 16
PROJ_ROWS = 512
OUT_ROWS = 1024
OUT_SUB_ROWS = 256
VMEM_LIMIT = 48 * 1024 * 1024

NT_DIMS = (((1,), (1,)), ((), ()))

f32 = jnp.float32
bf16 = jnp.bfloat16


def _split_bf16(a):
    hi = a.astype(bf16)
    return hi, (a - hi.astype(f32)).astype(bf16)


def _mod_kernel(c_ref, w_ref, b_ref, o_ref):
    c_hi, c_lo = _split_bf16(c_ref[...])
    w_hi, w_lo = _split_bf16(w_ref[...])
    dot = lambda a, b: jnp.dot(a, b, preferred_element_type=f32)
    o_ref[...] = dot(c_hi, w_hi) + (dot(c_hi, w_lo) + dot(c_lo, w_hi)) + b_ref[...]


def _modulation(c, w_ada, b_ada):
    B, D = c.shape
    N = w_ada.shape[1]
    bn = 512
    return pl.pallas_call(
        _mod_kernel,
        grid=(N // bn,),
        in_specs=[pl.BlockSpec((B, D), lambda n: (0, 0)),
                  pl.BlockSpec((D, bn), lambda n: (0, n)),
                  pl.BlockSpec((1, bn), lambda n: (0, n))],
        out_specs=pl.BlockSpec((B, bn), lambda n: (0, n)),
        out_shape=jax.ShapeDtypeStruct((B, N), f32),
        name="adaln_mod",
    )(c, w_ada, b_ada.reshape(1, N))


def _modulated(x_ref, mod_ref):
    shift = mod_ref[0, 0:1, :]
    scale = mod_ref[0, 1:2, :]
    return (x_ref[0] * (1.0 + scale) + shift).astype(bf16)


def _proj_kernel(x_ref, mod_ref, cos_ref, sin_ref, w_ref,
                 qa_ref, ka_ref, va_ref, ga_ref, qb_ref, kb_ref, vb_ref):
    tm = x_ref.shape[1]
    h = _modulated(x_ref, mod_ref)
    cos = cos_ref[...]
    sin = sin_ref[...]
    lane = lax.broadcasted_iota(jnp.int32, (tm, LANES), 1)
    first_half = (lane & (HEAD_DIM // 2)) == 0

    def rope(t):
        partner = jnp.where(first_half,
                            pltpu.roll(t, LANES - HEAD_DIM // 2, 1),
                            pltpu.roll(t, HEAD_DIM // 2, 1))
        return t * cos + partner * sin

    targets = (qa_ref, ka_ref, va_ref, ga_ref, qb_ref, kb_ref, vb_ref)
    kinds = ("q", "k", "v", "g", "q", "k", "v")
    for c in range(len(targets)):
        acc = jnp.dot(h, w_ref[:, c * D_BRANCH:(c + 1) * D_BRANCH],
                      preferred_element_type=f32)
        if kinds[c] == "g":
            ga_ref[0] = acc.astype(bf16)
            continue
        for gi in range(N_GROUPS):
            t = acc[:, gi * LANES:(gi + 1) * LANES]
            if kinds[c] == "q":
                t = rope(t) * Q_SCALE
            elif kinds[c] == "k":
                t = rope(t)
            targets[c][0, gi] = t.astype(bf16)


def _projection(x, mod3, cos_t, sin_t, w_in_bf):
    B, S, D = x.shape
    tm = PROJ_ROWS
    n_s = S // tm
    qkv_shape = jax.ShapeDtypeStruct((B, N_GROUPS, S, LANES), bf16)
    qkv_spec = pl.BlockSpec((1, N_GROUPS, tm, LANES), lambda b, s: (b, 0, s, 0))
    g_shape = jax.ShapeDtypeStruct((B, S, D_BRANCH), bf16)
    g_spec = pl.BlockSpec((1, tm, D_BRANCH), lambda b, s: (b, s, 0))
    return pl.pallas_call(
        _proj_kernel,
        grid=(B, n_s),
        in_specs=[pl.BlockSpec((1, tm, D), lambda b, s: (b, s, 0)),
                  pl.BlockSpec((1, 3, D), lambda b, s: (b, 0, 0)),
                  pl.BlockSpec((tm, LANES), lambda b, s: (s, 0)),
                  pl.BlockSpec((tm, LANES), lambda b, s: (s, 0)),
                  pl.BlockSpec((D, 7 * D_BRANCH), lambda b, s: (0, 0))],
        out_specs=[qkv_spec] * 3 + [g_spec] + [qkv_spec] * 3,
        out_shape=[qkv_shape] * 3 + [g_shape] + [qkv_shape] * 3,
        compiler_params=pltpu.CompilerParams(
            dimension_semantics=("arbitrary", "arbitrary"), vmem_limit_bytes=VMEM_LIMIT),
        name="in_proj_rope",
    )(x, mod3, cos_t, sin_t, w_in_bf)


def _own_lanes(h, shape):
    lane = lax.broadcasted_iota(jnp.int32, shape, len(shape) - 1)
    return (lane < HEAD_DIM) if h == 0 else (lane >= HEAD_DIM)


def _own_rows(h, shape):
    row = lax.broadcasted_iota(jnp.int32, shape, 0)
    return (row < HEAD_DIM) if h == 0 else (row >= HEAD_DIM)


def _fill_v_aug_t(v_ref, vaugt_ref):
    vt = v_ref[0, 0].astype(f32).T
    for h in range(HEADS_PER_GROUP):
        vaugt_ref[h] = jnp.where(_own_rows(h, vt.shape), vt, 1.0).astype(bf16)


class _Rows:
    def __init__(self, i, h, key_bias, query_bias, q_ref, k_ref, vaugt_ref, o_ref):
        self.i, self.h = i, h
        self.key_bias, self.query_bias = key_bias, query_bias
        self.q_ref, self.k_ref, self.vaugt_ref, self.o_ref = q_ref, k_ref, vaugt_ref, o_ref
        self.qh = self.m = self.acc = None
        self.todo = [i] + list(range(i))
        self.queue = []


def _online_tile(u, j):
    T = ATT_TILE
    if u.qh is None:
        q = u.q_ref[0, 0, u.i * T:(u.i + 1) * T, :]
        u.qh = jnp.where(_own_lanes(u.h, q.shape), q, jnp.zeros_like(q))
    s = lax.dot_general(u.k_ref[0, 0, j * T:(j + 1) * T, :], u.qh, NT_DIMS,
                        preferred_element_type=f32)
    kb = u.key_bias(j)
    if kb is not None:
        s = s + kb
    t = jnp.max(s.reshape(T // 8, 8, T), axis=0)
    qb = u.query_bias(j)
    if qb is not None:
        t = t + qb
    tmax = jnp.max(t, axis=0, keepdims=True)
    m_new = tmax if u.m is None else jnp.maximum(u.m, tmax)
    p = jnp.exp2(s - (m_new if qb is None else m_new - qb)).astype(bf16)
    alpha = None if u.m is None else jnp.exp2(u.m - m_new)
    u.m = m_new
    u.queue.append((j, p, alpha))


def _online_value(u):
    T = ATT_TILE
    j, p, alpha = u.queue.pop(0)
    pv = jnp.dot(u.vaugt_ref[u.h, :, j * T:(j + 1) * T], p, preferred_element_type=f32)
    u.acc = pv if alpha is None else u.acc * alpha + pv


def _run_rows(units):
    T = ATT_TILE
    done = {}
    pending = list(units)
    active = []

    def finish(u):
        den_row = HEAD_DIM if u.h == 0 else 0
        heads = done.setdefault((id(u.o_ref), u.i), {})
        heads[u.h] = u.acc / u.acc[den_row:den_row + 1, :]
        if len(heads) == HEADS_PER_GROUP:
            out_t = jnp.where(_own_rows(0, heads[0].shape), heads[0], heads[1])
            u.o_ref[0, 0, u.i * T:(u.i + 1) * T, :] = out_t.T.astype(u.o_ref.dtype)
            del done[(id(u.o_ref), u.i)]

    while pending or active:
        while len(active) < ONLINE_WINDOW and pending:
            active.append(pending.pop(0))
        lagging = [u for u in active if len(u.queue) >= VALUE_LAG or (u.queue and not u.todo)]
        for u in active:
            if u.todo:
                _online_tile(u, u.todo.pop(0))
        for u in lagging:
            _online_value(u)
        for u in list(active):
            if not u.todo and not u.queue:
                active.remove(u)
                finish(u)


def _dilated_units(periodic_offsets, q_ref, k_ref, slab_ref, o_ref, vaugt_ref):
    T = ATT_TILE
    n_t = q_ref.shape[2] // T

    def key_bias(i, j):
        row0 = (n_t - 1 - i + j) * T
        if i - j in periodic_offsets:
            return jnp.tile(slab_ref[row0:row0 + SLAB_PERIOD, :], (T // SLAB_PERIOD, 1))
        return slab_ref[row0:row0 + T, :]

    return [_Rows(i, h, lambda j, i=i: key_bias(i, j), lambda j: None,
                  q_ref, k_ref, vaugt_ref, o_ref)
            for i in range(n_t) for h in range(HEADS_PER_GROUP)]


def _moba_units(q_ref, k_ref, o_ref, vaugt_ref, kmean_ref):
    T = ATT_TILE
    S = q_ref.shape[2]
    n_t = S // T

    kmean_ref[...] = jnp.zeros_like(kmean_ref)
    for j in range(n_t):
        kj = k_ref[0, 0, j * T:(j + 1) * T, :]
        kmean_ref[j:j + 1, :] = jnp.sum(kj.astype(f32), axis=0, keepdims=True) * (1.0 / T)
    kmean = kmean_ref[...]
    km_parts = []
    for h in range(HEADS_PER_GROUP):
        km_parts += list(_split_bf16(jnp.where(_own_lanes(h, kmean.shape), kmean, 0.0)))
    km_rows = kmean.shape[0]
    c0 = min((MOBA_TOPK + 1) * T, S - T)
    W = S - c0
    gates = lax.dot_general(jnp.concatenate(km_parts, axis=0), q_ref[0, 0, c0:S, :], NT_DIMS,
                            preferred_element_type=f32)

    key_i = lax.broadcasted_iota(jnp.int32, (T, T), 0)
    qry_i = lax.broadcasted_iota(jnp.int32, (T, T), 1)
    causal_bias = jnp.where(key_i <= qry_i, 0.0, NEG_INF).astype(f32)

    blk = lax.broadcasted_iota(jnp.int32, (8, W), 0)
    own_blk = (lax.broadcasted_iota(jnp.int32, (8, W), 1) + c0) // T
    past = blk < own_blk
    drop = []
    for h in range(HEADS_PER_GROUP):
        hi0 = 2 * h * km_rows
        gate = gates[hi0:hi0 + 8] + gates[hi0 + km_rows:hi0 + km_rows + 8]
        cnt = jnp.zeros((8, W), f32)
        for jp in range(n_t - 1):
            gj = gate[jp:jp + 1, :]
            ahead = (gj > gate) | ((gj == gate) & (jp < blk))
            cnt = cnt + jnp.where(ahead & (jp < own_blk), 1.0, 0.0)
        drop.append(jnp.where(past & (cnt >= float(MOBA_TOPK)), NEG_INF, 0.0))

    def query_bias(i, h, j):
        if j == i or i * T < c0:
            return None
        return drop[h][j:j + 1, i * T - c0:(i + 1) * T - c0]

    return [_Rows(i, h,
                  lambda j, i=i: causal_bias if j == i else None,
                  lambda j, i=i, h=h: query_bias(i, h, j),
                  q_ref, k_ref, vaugt_ref, o_ref)
            for i in range(n_t) for h in range(HEADS_PER_GROUP)]


def _mixers_kernel(periodic_offsets, qa_ref, ka_ref, va_ref, qb_ref, kb_ref, vb_ref, slab_ref,
                   oa_ref, ob_ref, vaugt_a_ref, vaugt_b_ref, kmean_ref):
    _fill_v_aug_t(va_ref, vaugt_a_ref)
    _fill_v_aug_t(vb_ref, vaugt_b_ref)
    dilated = _dilated_units(periodic_offsets, qa_ref, ka_ref, slab_ref, oa_ref, vaugt_a_ref)
    moba = _moba_units(qb_ref, kb_ref, ob_ref, vaugt_b_ref, kmean_ref)
    _run_rows([u for pair in zip(dilated, moba) for u in pair])


def _dilated_bias_slab(n_t):
    T = ATT_TILE
    key = np.arange(T)[:, None]
    qry = np.arange(T)[None, :]
    slab = np.empty((n_t * T, T), np.float32)
    for b in range(n_t):
        dist = (n_t - 1 - b) * T + qry - key
        mult = np.zeros((T, T), np.float64)
        for window, dil in DILATED_PATTERNS:
            mult += (dist >= 0) & (dist <= window) & (dist % dil == 0)
        slab[b * T:(b + 1) * T, :] = np.where(mult > 0, np.log2(np.maximum(mult, 1.0)), NEG_INF)
    periodic = frozenset(
        n_t - 1 - b for b in range(n_t)
        if np.array_equal(slab[b * T:(b + 1) * T],
                          np.tile(slab[b * T:b * T + SLAB_PERIOD], (T // SLAB_PERIOD, 1))))
    return jnp.asarray(slab), periodic


def _mixers(qa, ka, va, qb, kb, vb):
    B, G, S, _ = qa.shape
    assert ATT_TILE == MOBA_BLOCK and S // ATT_TILE <= 8
    slab, periodic = _dilated_bias_slab(S // ATT_TILE)
    blk = pl.BlockSpec((1, 1, S, LANES), lambda b, g: (b, g, 0, 0))
    vaugt = pltpu.VMEM((HEADS_PER_GROUP, LANES, S), bf16)
    out = jax.ShapeDtypeStruct(qa.shape, bf16)
    return pl.pallas_call(
        functools.partial(_mixers_kernel, periodic),
        grid=(B, G),
        in_specs=[blk] * 6 + [pl.BlockSpec(slab.shape, lambda b, g: (0, 0))],
        out_specs=[blk, blk],
        out_shape=[out, out],
        scratch_shapes=[vaugt, vaugt,
                        pltpu.VMEM((16, LANES), f32)],
        compiler_params=pltpu.CompilerParams(
            dimension_semantics=("arbitrary", "arbitrary"), vmem_limit_bytes=VMEM_LIMIT),
        name="mixers_attn",
    )(qa, ka, va, qb, kb, vb, slab)


def _out_kernel(oa_ref, ob_ref, ga_ref, x_ref, mod_ref, wgb_ref, w_ref, lng_ref, lnb_ref,
                o_ref, og_ref):
    tm = x_ref.shape[1]
    shift, scale, gate = mod_ref[0, 0:1, :], mod_ref[0, 1:2, :], mod_ref[0, 2:3, :]

    def gate_stage(rows):
        h = (x_ref[0, rows, :] * (1.0 + scale) + shift).astype(bf16)
        g_b = jnp.dot(h, wgb_ref[...], preferred_element_type=f32)
        for br, (src, g) in enumerate(((oa_ref, ga_ref[0, rows, :].astype(f32)), (ob_ref, g_b))):
            half_g = 0.5 * g
            silu = half_g + half_g * jnp.tanh(half_g)
            for gi in range(N_GROUPS):
                lo = br * D_BRANCH + gi * LANES
                og_ref[rows, lo:lo + LANES] = (src[0, gi, rows, :].astype(f32)
                                               * silu[:, gi * LANES:(gi + 1) * LANES]).astype(bf16)

    def out_stage(rows):
        y = jnp.dot(og_ref[rows, :], w_ref[...], preferred_element_type=f32)
        z = DEEPNORM_ALPHA * x_ref[0, rows, :] + gate * y
        mu = jnp.mean(z, axis=-1, keepdims=True)
        zc = z - mu
        var = jnp.mean(zc * zc, axis=-1, keepdims=True)
        o_ref[0, rows, :] = zc * lax.rsqrt(var + LN_EPS) * lng_ref[...] + lnb_ref[...]

    subs = [slice(r, r + OUT_SUB_ROWS) for r in range(0, tm, OUT_SUB_ROWS)]
    for idx in range(len(subs) + 1):
        if idx < len(subs):
            gate_stage(subs[idx])
        if idx >= 1:
            out_stage(subs[idx - 1])


def _out_projection(o_a, o_b, g_a, x, mod3, w_in_bf, w_out_bf, ln_g, ln_b):
    B, S, D = x.shape
    tm = OUT_ROWS
    att_spec = pl.BlockSpec((1, N_GROUPS, tm, LANES), lambda b, s: (b, 0, s, 0))
    row_spec = pl.BlockSpec((1, tm, D), lambda b, s: (b, s, 0))
    vec_spec = pl.BlockSpec((1, D), lambda b, s: (0, 0))
    return pl.pallas_call(
        _out_kernel,
        grid=(B, S // tm),
        in_specs=[att_spec, att_spec,
                  pl.BlockSpec((1, tm, D_BRANCH), lambda b, s: (b, s, 0)),
                  row_spec,
                  pl.BlockSpec((1, 3, D), lambda b, s: (b, 0, 0)),
                  pl.BlockSpec((D, D_BRANCH), lambda b, s: (0, 7)),
                  pl.BlockSpec(w_out_bf.shape, lambda b, s: (0, 0)),
                  vec_spec, vec_spec],
        out_specs=row_spec,
        out_shape=jax.ShapeDtypeStruct((B, S, D), f32),
        scratch_shapes=[pltpu.VMEM((tm, 2 * D_BRANCH), bf16)],
        compiler_params=pltpu.CompilerParams(
            dimension_semantics=("arbitrary", "arbitrary"), vmem_limit_bytes=VMEM_LIMIT),
        name="out_proj_ln",
    )(o_a, o_b, g_a, x, mod3, w_in_bf, w_out_bf, ln_g.reshape(1, D), ln_b.reshape(1, D))


def _rope_tables(S):
    half = HEAD_DIM // 2
    inv = ROPE_THETA ** (-jnp.arange(half, dtype=f32) / half)
    ang = jnp.arange(S, dtype=jnp.int32).astype(f32)[:, None] * inv[None, :]
    cos, sin = jnp.cos(ang), jnp.sin(ang)
    cos_head = jnp.concatenate([cos, cos], axis=-1)
    sin_head = jnp.concatenate([-sin, sin], axis=-1)
    return (jnp.tile(cos_head, (1, HEADS_PER_GROUP)), jnp.tile(sin_head, (1, HEADS_PER_GROUP)))


def kernel(x, c, w_in, w_out, w_ada, b_ada, ln_g, ln_b):
    B, S, D = x.shape
    cos_t, sin_t = _rope_tables(S)
    for layer in range(w_in.shape[0]):
        mod3 = _modulation(c, w_ada[layer], b_ada[layer]).reshape(B, 3, D)
        w_in_bf = w_in[layer].astype(bf16)
        qa, ka, va, g_a, qb, kb, vb = _projection(x, mod3, cos_t, sin_t, w_in_bf)
        o_a, o_b = _mixers(qa, ka, va, qb, kb, vb)
        x = _out_projection(o_a, o_b, g_a, x, mod3, w_in_bf, w_out[layer].astype(bf16),
                            ln_g[layer], ln_b[layer])
    return x
```

```python
import functools

import numpy as np
import jax
import jax.numpy as jnp
from jax import lax
from jax.experimental import pallas as pl
from jax.experimental.pallas import tpu as pltpu

HEAD_DIM = 64
N_HEADS = 8
D_BRANCH = N_HEADS * HEAD_DIM
DILATED_PATTERNS = ((128, 1), (512, 4), (2048, 16))
MOBA_BLOCK = 256
MOBA_TOPK = 3
ROPE_THETA = 10000.0
LN_EPS = 1e-5
NEG_INF = -1e30
DEPTH = 1
DEEPNORM_ALPHA = (2.0 * DEPTH) ** 0.25
LOG2E = 1.4426950408889634
Q_SCALE = HEAD_DIM ** -0.5 * LOG2E

LANES = 128
HEADS_PER_GROUP = LANES // HEAD_DIM
N_GROUPS = N_HEADS // HEADS_PER_GROUP
ATT_TILE = 256
ONLINE_WINDOW = 3
STEP_TILES = 1
VALUE_LAG = 2
SLAB_PERIOD = 16
PROJ_ROWS = 1024
OUT_ROWS = 1024
OUT_SUB_ROWS = 256
VMEM_LIMIT = 48 * 1024 * 1024

NT_DIMS = (((1,), (1,)), ((), ()))

f32 = jnp.float32
bf16 = jnp.bfloat16


def _split_bf16(a):
    hi = a.astype(bf16)
    return hi, (a - hi.astype(f32)).astype(bf16)


def _mod_kernel(c_ref, w_ref, b_ref, o_ref):
    c_hi, c_lo = _split_bf16(c_ref[...])
    w_hi, w_lo = _split_bf16(w_ref[...])
    dot = lambda a, b: jnp.dot(a, b, preferred_element_type=f32)
    o_ref[...] = dot(c_hi, w_hi) + (dot(c_hi, w_lo) + dot(c_lo, w_hi)) + b_ref[...]


def _modulation(c, w_ada, b_ada):
    B, D = c.shape
    N = w_ada.shape[1]
    bn = 512
    return pl.pallas_call(
        _mod_kernel,
        grid=(N // bn,),
        in_specs=[pl.BlockSpec((B, D), lambda n: (0, 0)),
                  pl.BlockSpec((D, bn), lambda n: (0, n)),
                  pl.BlockSpec((1, bn), lambda n: (0, n))],
        out_specs=pl.BlockSpec((B, bn), lambda n: (0, n)),
        out_shape=jax.ShapeDtypeStruct((B, N), f32),
        name="adaln_mod",
    )(c, w_ada, b_ada.reshape(1, N))


def _modulated(x_ref, mod_ref):
    shift = mod_ref[0, 0:1, :]
    scale = mod_ref[0, 1:2, :]
    return (x_ref[0] * (1.0 + scale) + shift).astype(bf16)


def _proj_kernel(x_ref, mod_ref, cos_ref, sin_ref, w_ref,
                 qa_ref, ka_ref, va_ref, ga_ref, qb_ref, kb_ref, vb_ref):
    tm = x_ref.shape[1]
    h = _modulated(x_ref, mod_ref)
    cos = cos_ref[...]
    sin = sin_ref[...]
    lane = lax.broadcasted_iota(jnp.int32, (tm, LANES), 1)
    first_half = (lane & (HEAD_DIM // 2)) == 0

    def rope(t):
        partner = jnp.where(first_half,
                            pltpu.roll(t, LANES - HEAD_DIM // 2, 1),
                            pltpu.roll(t, HEAD_DIM // 2, 1))
        return t * cos + partner * sin

    targets = (qa_ref, ka_ref, va_ref, ga_ref, qb_ref, kb_ref, vb_ref)
    kinds = ("q", "k", "v", "g", "q", "k", "v")
    for c in range(len(targets)):
        acc = jnp.dot(h, w_ref[:, c * D_BRANCH:(c + 1) * D_BRANCH],
                      preferred_element_type=f32)
        if kinds[c] == "g":
            ga_ref[0] = acc.astype(bf16)
            continue
        for gi in range(N_GROUPS):
            t = acc[:, gi * LANES:(gi + 1) * LANES]
            if kinds[c] == "q":
                t = rope(t) * Q_SCALE
            elif kinds[c] == "k":
                t = rope(t)
            targets[c][0, gi] = t.astype(bf16)


def _projection(x, mod3, cos_t, sin_t, w_in_bf):
    B, S, D = x.shape
    tm = PROJ_ROWS
    n_s = S // tm
    qkv_shape = jax.ShapeDtypeStruct((B, N_GROUPS, S, LANES), bf16)
    qkv_spec = pl.BlockSpec((1, N_GROUPS, tm, LANES), lambda b, s: (b, 0, s, 0))
    g_shape = jax.ShapeDtypeStruct((B, S, D_BRANCH), bf16)
    g_spec = pl.BlockSpec((1, tm, D_BRANCH), lambda b, s: (b, s, 0))
    return pl.pallas_call(
        _proj_kernel,
        grid=(B, n_s),
        in_specs=[pl.BlockSpec((1, tm, D), lambda b, s: (b, s, 0)),
                  pl.BlockSpec((1, 3, D), lambda b, s: (b, 0, 0)),
                  pl.BlockSpec((tm, LANES), lambda b, s: (s, 0)),
                  pl.BlockSpec((tm, LANES), lambda b, s: (s, 0)),
                  pl.BlockSpec((D, 7 * D_BRANCH), lambda b, s: (0, 0))],
        out_specs=[qkv_spec] * 3 + [g_spec] + [qkv_spec] * 3,
        out_shape=[qkv_shape] * 3 + [g_shape] + [qkv_shape] * 3,
        compiler_params=pltpu.CompilerParams(
            dimension_semantics=("arbitrary", "arbitrary"), vmem_limit_bytes=VMEM_LIMIT),
        name="in_proj_rope",
    )(x, mod3, cos_t, sin_t, w_in_bf)


def _own_lanes(h, shape):
    lane = lax.broadcasted_iota(jnp.int32, shape, len(shape) - 1)
    return (lane < HEAD_DIM) if h == 0 else (lane >= HEAD_DIM)


def _own_rows(h, shape):
    row = lax.broadcasted_iota(jnp.int32, shape, 0)
    return (row < HEAD_DIM) if h == 0 else (row >= HEAD_DIM)


def _fill_v_aug_t(v_ref, vaugt_ref):
    vt = v_ref[0, 0].astype(f32).T
    for h in range(HEADS_PER_GROUP):
        vaugt_ref[h] = jnp.where(_own_rows(h, vt.shape), vt, 1.0).astype(bf16)


class _Rows:
    def __init__(self, i, h, key_bias, query_bias, q_ref, k_ref, vaugt_ref, o_ref):
        self.i, self.h = i, h
        self.key_bias, self.query_bias = key_bias, query_bias
        self.q_ref, self.k_ref, self.vaugt_ref, self.o_ref = q_ref, k_ref, vaugt_ref, o_ref
        self.qh = self.m = self.acc = None
        first = list(range(max(0, i + 1 - STEP_TILES), i + 1))
        rest = list(range(first[0]))
        self.todo = [first] + [rest[a:a + STEP_TILES] for a in range(0, len(rest), STEP_TILES)]
        self.queue = []


def _online_scores(u, js):
    T = ATT_TILE
    if u.qh is None:
        q = u.q_ref[0, 0, u.i * T:(u.i + 1) * T, :]
        u.qh = jnp.where(_own_lanes(u.h, q.shape), q, jnp.zeros_like(q))
    s = lax.dot_general(u.k_ref[0, 0, js[0] * T:(js[-1] + 1) * T, :], u.qh, NT_DIMS,
                        preferred_element_type=f32)
    tiles, tmax = [], None
    for n, j in enumerate(js):
        sj = s[n * T:(n + 1) * T]
        kb = u.key_bias(j)
        if kb is not None:
            sj = sj + kb
        t = jnp.max(sj.reshape(T // 8, 8, T), axis=0)
        qb = u.query_bias(j)
        if qb is not None:
            t = t + qb
        tiles.append((sj, qb))
        tmax = t if tmax is None else jnp.maximum(tmax, t)
    tmax = jnp.max(tmax, axis=0, keepdims=True)
    m_new = tmax if u.m is None else jnp.maximum(u.m, tmax)
    ps = [jnp.exp2(sj - (m_new if qb is None else m_new - qb)).astype(bf16) for sj, qb in tiles]
    p = ps[0] if len(ps) == 1 else jnp.concatenate(ps, axis=0)
    alpha = None if u.m is None else jnp.exp2(u.m - m_new)
    u.m = m_new
    u.queue.append((js, p, alpha))


def _online_value(u):
    T = ATT_TILE
    js, p, alpha = u.queue.pop(0)
    pv = jnp.dot(u.vaugt_ref[u.h, :, js[0] * T:(js[-1] + 1) * T], p,
                 preferred_element_type=f32)
    u.acc = pv if alpha is None else u.acc * alpha + pv


def _run_rows(units):
    T = ATT_TILE
    done = {}
    pending = list(units)
    active = []

    def finish(u):
        den_row = HEAD_DIM if u.h == 0 else 0
        heads = done.setdefault((id(u.o_ref), u.i), {})
        heads[u.h] = u.acc / u.acc[den_row:den_row + 1, :]
        if len(heads) == HEADS_PER_GROUP:
            out_t = jnp.where(_own_rows(0, heads[0].shape), heads[0], heads[1])
            u.o_ref[0, 0, u.i * T:(u.i + 1) * T, :] = out_t.T.astype(u.o_ref.dtype)
            del done[(id(u.o_ref), u.i)]

    while pending or active:
        while len(active) < ONLINE_WINDOW and pending:
            active.append(pending.pop(0))
        lagging = [u for u in active if len(u.queue) >= VALUE_LAG or (u.queue and not u.todo)]
        for u in active:
            if u.todo:
                _online_scores(u, u.todo.pop(0))
        for u in lagging:
            _online_value(u)
        for u in list(active):
            if not u.todo and not u.queue:
                active.remove(u)
                finish(u)


def _dilated_units(periodic_offsets, q_ref, k_ref, slab_ref, o_ref, vaugt_ref):
    T = ATT_TILE
    n_t = q_ref.shape[2] // T

    def key_bias(i, j):
        row0 = (n_t - 1 - i + j) * T
        if i - j in periodic_offsets:
            return jnp.tile(slab_ref[row0:row0 + SLAB_PERIOD, :], (T // SLAB_PERIOD, 1))
        return slab_ref[row0:row0 + T, :]

    return [_Rows(i, h, lambda j, i=i: key_bias(i, j), lambda j: None,
                  q_ref, k_ref, vaugt_ref, o_ref)
            for i in range(n_t) for h in range(HEADS_PER_GROUP)]


def _moba_units(q_ref, k_ref, o_ref, vaugt_ref, kmean_ref):
    T = ATT_TILE
    S = q_ref.shape[2]
    n_t = S // T

    kmean_ref[...] = jnp.zeros_like(kmean_ref)
    for j in range(n_t):
        kj = k_ref[0, 0, j * T:(j + 1) * T, :]
        kmean_ref[j:j + 1, :] = jnp.sum(kj.astype(f32), axis=0, keepdims=True) * (1.0 / T)
    kmean = kmean_ref[...]
    km_parts = []
    for h in range(HEADS_PER_GROUP):
        km_parts += list(_split_bf16(jnp.where(_own_lanes(h, kmean.shape), kmean, 0.0)))
    km_rows = kmean.shape[0]
    c0 = min((MOBA_TOPK + 1) * T, S - T)
    W = S - c0
    gates = lax.dot_general(jnp.concatenate(km_parts, axis=0), q_ref[0, 0, c0:S, :], NT_DIMS,
                            preferred_element_type=f32)

    key_i = lax.broadcasted_iota(jnp.int32, (T, T), 0)
    qry_i = lax.broadcasted_iota(jnp.int32, (T, T), 1)
    causal_bias = jnp.where(key_i <= qry_i, 0.0, NEG_INF).astype(f32)

    blk = lax.broadcasted_iota(jnp.int32, (8, W), 0)
    own_blk = (lax.broadcasted_iota(jnp.int32, (8, W), 1) + c0) // T
    past = blk < own_blk
    drop = []
    for h in range(HEADS_PER_GROUP):
        hi0 = 2 * h * km_rows
        gate = gates[hi0:hi0 + 8] + gates[hi0 + km_rows:hi0 + km_rows + 8]
        cnt = jnp.zeros((8, W), f32)
        for jp in range(n_t - 1):
            gj = gate[jp:jp + 1, :]
            ahead = (gj > gate) | ((gj == gate) & (jp < blk))
            cnt = cnt + jnp.where(ahead & (jp < own_blk), 1.0, 0.0)
        drop.append(jnp.where(past & (cnt >= float(MOBA_TOPK)), NEG_INF, 0.0))

    def query_bias(i, h, j):
        if j == i or i * T < c0:
            return None
        return drop[h][j:j + 1, i * T - c0:(i + 1) * T - c0]

    return [_Rows(i, h,
                  lambda j, i=i: causal_bias if j == i else None,
                  lambda j, i=i, h=h: query_bias(i, h, j),
                  q_ref, k_ref, vaugt_ref, o_ref)
            for i in range(n_t) for h in range(HEADS_PER_GROUP)]


def _mixers_kernel(periodic_offsets, qa_ref, ka_ref, va_ref, qb_ref, kb_ref, vb_ref, slab_ref,
                   oa_ref, ob_ref, vaugt_a_ref, vaugt_b_ref, kmean_ref):
    _fill_v_aug_t(va_ref, vaugt_a_ref)
    _fill_v_aug_t(vb_ref, vaugt_b_ref)
    dilated = _dilated_units(periodic_offsets, qa_ref, ka_ref, slab_ref, oa_ref, vaugt_a_ref)
    moba = _moba_units(qb_ref, kb_ref, ob_ref, vaugt_b_ref, kmean_ref)
    _run_rows([u for pair in zip(dilated, moba) for u in pair])


def _dilated_bias_slab(n_t):
    T = ATT_TILE
    key = np.arange(T)[:, None]
    qry = np.arange(T)[None, :]
    slab = np.empty((n_t * T, T), np.float32)
    for b in range(n_t):
        dist = (n_t - 1 - b) * T + qry - key
        mult = np.zeros((T, T), np.float64)
        for window, dil in DILATED_PATTERNS:
            mult += (dist >= 0) & (dist <= window) & (dist % dil == 0)
        slab[b * T:(b + 1) * T, :] = np.where(mult > 0, np.log2(np.maximum(mult, 1.0)), NEG_INF)
    periodic = frozenset(
        n_t - 1 - b for b in range(n_t)
        if np.array_equal(slab[b * T:(b + 1) * T],
                          np.tile(slab[b * T:b * T + SLAB_PERIOD], (T // SLAB_PERIOD, 1))))
    return jnp.asarray(slab), periodic


def _mixers(qa, ka, va, qb, kb, vb):
    B, G, S, _ = qa.shape
    assert ATT_TILE == MOBA_BLOCK and S // ATT_TILE <= 8
    slab, periodic = _dilated_bias_slab(S // ATT_TILE)
    blk = pl.BlockSpec((1, 1, S, LANES), lambda b, g: (b, g, 0, 0))
    vaugt = pltpu.VMEM((HEADS_PER_GROUP, LANES, S), bf16)
    out = jax.ShapeDtypeStruct(qa.shape, bf16)
    return pl.pallas_call(
        functools.partial(_mixers_kernel, periodic),
        grid=(B, G),
        in_specs=[blk] * 6 + [pl.BlockSpec(slab.shape, lambda b, g: (0, 0))],
        out_specs=[blk, blk],
        out_shape=[out, out],
        scratch_shapes=[vaugt, vaugt,
                        pltpu.VMEM((16, LANES), f32)],
        compiler_params=pltpu.CompilerParams(
            dimension_semantics=("arbitrary", "arbitrary"), vmem_limit_bytes=VMEM_LIMIT),
        name="mixers_attn",
    )(qa, ka, va, qb, kb, vb, slab)


def _out_kernel(oa_ref, ob_ref, ga_ref, x_ref, mod_ref, wgb_ref, w_ref, lng_ref, lnb_ref,
                o_ref, og_ref):
    tm = x_ref.shape[1]
    shift, scale, gate = mod_ref[0, 0:1, :], mod_ref[0, 1:2, :], mod_ref[0, 2:3, :]

    def gate_stage(rows):
        h = (x_ref[0, rows, :] * (1.0 + scale) + shift).astype(bf16)
        g_b = jnp.dot(h, wgb_ref[...], preferred_element_type=f32)
        for br, (src, g) in enumerate(((oa_ref, ga_ref[0, rows, :].astype(f32)), (ob_ref, g_b))):
            half_g = 0.5 * g
            silu = half_g + half_g * jnp.tanh(half_g)
            for gi in range(N_GROUPS):
                lo = br * D_BRANCH + gi * LANES
                og_ref[rows, lo:lo + LANES] = (src[0, gi, rows, :].astype(f32)
                                               * silu[:, gi * LANES:(gi + 1) * LANES]).astype(bf16)

    def out_stage(rows):
        y = jnp.dot(og_ref[rows, :], w_ref[...], preferred_element_type=f32)
        z = DEEPNORM_ALPHA * x_ref[0, rows, :] + gate * y
        mu = jnp.mean(z, axis=-1, keepdims=True)
        zc = z - mu
        var = jnp.mean(zc * zc, axis=-1, keepdims=True)
        o_ref[0, rows, :] = zc * lax.rsqrt(var + LN_EPS) * lng_ref[...] + lnb_ref[...]

    subs = [slice(r, r + OUT_SUB_ROWS) for r in range(0, tm, OUT_SUB_ROWS)]
    for idx in range(len(subs) + 1):
        if idx < len(subs):
            gate_stage(subs[idx])
        if idx >= 1:
            out_stage(subs[idx - 1])


def _out_projection(o_a, o_b, g_a, x, mod3, w_in_bf, w_out_bf, ln_g, ln_b):
    B, S, D = x.shape
    tm = OUT_ROWS
    att_spec = pl.BlockSpec((1, N_GROUPS, tm, LANES), lambda b, s: (b, 0, s, 0))
    row_spec = pl.BlockSpec((1, tm, D), lambda b, s: (b, s, 0))
    vec_spec = pl.BlockSpec((1, D), lambda b, s: (0, 0))
    return pl.pallas_call(
        _out_kernel,
        grid=(B, S // tm),
        in_specs=[att_spec, att_spec,
                  pl.BlockSpec((1, tm, D_BRANCH), lambda b, s: (b, s, 0)),
                  row_spec,
                  pl.BlockSpec((1, 3, D), lambda b, s: (b, 0, 0)),
                  pl.BlockSpec((D, D_BRANCH), lambda b, s: (0, 7)),
                  pl.BlockSpec(w_out_bf.shape, lambda b, s: (0, 0)),
                  vec_spec, vec_spec],
        out_specs=row_spec,
        out_shape=jax.ShapeDtypeStruct((B, S, D), f32),
        scratch_shapes=[pltpu.VMEM((tm, 2 * D_BRANCH), bf16)],
        compiler_params=pltpu.CompilerParams(
            dimension_semantics=("arbitrary", "arbitrary"), vmem_limit_bytes=VMEM_LIMIT),
        name="out_proj_ln",
    )(o_a, o_b, g_a, x, mod3, w_in_bf, w_out_bf, ln_g.reshape(1, D), ln_b.reshape(1, D))


def _rope_tables(S):
    half = HEAD_DIM // 2
    inv = ROPE_THETA ** (-jnp.arange(half, dtype=f32) / half)
    ang = jnp.arange(S, dtype=jnp.int32).astype(f32)[:, None] * inv[None, :]
    cos, sin = jnp.cos(ang), jnp.sin(ang)
    cos_head = jnp.concatenate([cos, cos], axis=-1)
    sin_head = jnp.concatenate([-sin, sin], axis=-1)
    return (jnp.tile(cos_head, (1, HEADS_PER_GROUP)), jnp.tile(sin_head, (1, HEADS_PER_GROUP)))


def kernel(x, c, w_in, w_out, w_ada, b_ada, ln_g, ln_b):
    B, S, D = x.shape
    cos_t, sin_t = _rope_tables(S)
    for layer in range(w_in.shape[0]):
        mod3 = _modulation(c, w_ada[layer], b_ada[layer]).reshape(B, 3, D)
        w_in_bf = w_in[layer].astype(bf16)
        qa, ka, va, g_a, qb, kb, vb = _projection(x, mod3, cos_t, sin_t, w_in_bf)
        o_a, o_b = _mixers(qa, ka, va, qb, kb, vb)
        x = _out_projection(o_a, o_b, g_a, x, mod3, w_in_bf, w_out[layer].astype(bf16),
                            ln_g[layer], ln_b[layer])
    return x
```

```python
import functools

import numpy as np
import jax
import jax.numpy as jnp
from jax import lax
from jax.experimental import pallas as pl
from jax.experimental.pallas import tpu as pltpu

HEAD_DIM = 64
N_HEADS = 8
D_BRANCH = N_HEADS * HEAD_DIM
DILATED_PATTERNS = ((128, 1), (512, 4), (2048, 16))
MOBA_BLOCK = 256
MOBA_TOPK = 3
ROPE_THETA = 10000.0
LN_EPS = 1e-5
NEG_INF = -1e30
DEPTH = 1
DEEPNORM_ALPHA = (2.0 * DEPTH) ** 0.25
LOG2E = 1.4426950408889634
Q_SCALE = HEAD_DIM ** -0.5 * LOG2E

LANES = 128
HEADS_PER_GROUP = LANES // HEAD_DIM
N_GROUPS = N_HEADS // HEADS_PER_GROUP
ATT_TILE = 256
VAUG_ROWS = HEAD_DIM + 16
ONLINE_WINDOW = 3
STEP_TILES = 1
VALUE_LAG = 2
SLAB_PERIOD = 16
PROJ_ROWS = 1024
OUT_ROWS = 1024
OUT_SUB_ROWS = 256
VMEM_LIMIT = 48 * 1024 * 1024

NT_DIMS = (((1,), (1,)), ((), ()))

f32 = jnp.float32
bf16 = jnp.bfloat16


def _split_bf16(a):
    hi = a.astype(bf16)
    return hi, (a - hi.astype(f32)).astype(bf16)


def _mod_kernel(c_ref, w_ref, b_ref, o_ref):
    c_hi, c_lo = _split_bf16(c_ref[...])
    w_hi, w_lo = _split_bf16(w_ref[...])
    dot = lambda a, b: jnp.dot(a, b, preferred_element_type=f32)
    o_ref[...] = dot(c_hi, w_hi) + (dot(c_hi, w_lo) + dot(c_lo, w_hi)) + b_ref[...]


def _modulation(c, w_ada, b_ada):
    B, D = c.shape
    N = w_ada.shape[1]
    bn = 512
    return pl.pallas_call(
        _mod_kernel,
        grid=(N // bn,),
        in_specs=[pl.BlockSpec((B, D), lambda n: (0, 0)),
                  pl.BlockSpec((D, bn), lambda n: (0, n)),
                  pl.BlockSpec((1, bn), lambda n: (0, n))],
        out_specs=pl.BlockSpec((B, bn), lambda n: (0, n)),
        out_shape=jax.ShapeDtypeStruct((B, N), f32),
        name="adaln_mod",
    )(c, w_ada, b_ada.reshape(1, N))


def _modulated(x_ref, mod_ref):
    shift = mod_ref[0, 0:1, :]
    scale = mod_ref[0, 1:2, :]
    return (x_ref[0] * (1.0 + scale) + shift).astype(bf16)


def _proj_kernel(x_ref, mod_ref, cos_ref, sin_ref, w_ref,
                 qa_ref, ka_ref, va_ref, ga_ref, qb_ref, kb_ref, vb_ref):
    tm = x_ref.shape[1]
    h = _modulated(x_ref, mod_ref)
    cos = cos_ref[...]
    sin = sin_ref[...]
    lane = lax.broadcasted_iota(jnp.int32, (tm, LANES), 1)
    first_half = (lane & (HEAD_DIM // 2)) == 0

    def rope(t):
        partner = jnp.where(first_half,
                            pltpu.roll(t, LANES - HEAD_DIM // 2, 1),
                            pltpu.roll(t, HEAD_DIM // 2, 1))
        return t * cos + partner * sin

    targets = (qa_ref, ka_ref, va_ref, ga_ref, qb_ref, kb_ref, vb_ref)
    kinds = ("q", "k", "v", "g", "q", "k", "v")
    for c in range(len(targets)):
        acc = jnp.dot(h, w_ref[:, c * D_BRANCH:(c + 1) * D_BRANCH],
                      preferred_element_type=f32)
        if kinds[c] == "g":
            ga_ref[0] = acc.astype(bf16)
            continue
        for gi in range(N_GROUPS):
            t = acc[:, gi * LANES:(gi + 1) * LANES]
            if kinds[c] == "q":
                t = rope(t) * Q_SCALE
            elif kinds[c] == "k":
                t = rope(t)
            targets[c][0, gi] = t.astype(bf16)


def _projection(x, mod3, cos_t, sin_t, w_in_bf):
    B, S, D = x.shape
    tm = PROJ_ROWS
    n_s = S // tm
    qkv_shape = jax.ShapeDtypeStruct((B, N_GROUPS, S, LANES), bf16)
    qkv_spec = pl.BlockSpec((1, N_GROUPS, tm, LANES), lambda b, s: (b, 0, s, 0))
    g_shape = jax.ShapeDtypeStruct((B, S, D_BRANCH), bf16)
    g_spec = pl.BlockSpec((1, tm, D_BRANCH), lambda b, s: (b, s, 0))
    return pl.pallas_call(
        _proj_kernel,
        grid=(B, n_s),
        in_specs=[pl.BlockSpec((1, tm, D), lambda b, s: (b, s, 0)),
                  pl.BlockSpec((1, 3, D), lambda b, s: (b, 0, 0)),
                  pl.BlockSpec((tm, LANES), lambda b, s: (s, 0)),
                  pl.BlockSpec((tm, LANES), lambda b, s: (s, 0)),
                  pl.BlockSpec((D, 7 * D_BRANCH), lambda b, s: (0, 0))],
        out_specs=[qkv_spec] * 3 + [g_spec] + [qkv_spec] * 3,
        out_shape=[qkv_shape] * 3 + [g_shape] + [qkv_shape] * 3,
        compiler_params=pltpu.CompilerParams(
            dimension_semantics=("arbitrary", "arbitrary"), vmem_limit_bytes=VMEM_LIMIT),
        name="in_proj_rope",
    )(x, mod3, cos_t, sin_t, w_in_bf)


def _own_lanes(h, shape):
    lane = lax.broadcasted_iota(jnp.int32, shape, len(shape) - 1)
    return (lane < HEAD_DIM) if h == 0 else (lane >= HEAD_DIM)


def _fill_v_aug_t(v_ref, vaugt_ref):
    vt = v_ref[0, 0].astype(f32).T
    ones = jnp.ones((VAUG_ROWS - HEAD_DIM, vt.shape[1]), f32)
    for h in range(HEADS_PER_GROUP):
        head = vt[h * HEAD_DIM:(h + 1) * HEAD_DIM]
        vaugt_ref[h] = jnp.concatenate([head, ones], axis=0).astype(bf16)


class _Rows:
    def __init__(self, i, h, key_bias, query_bias, q_ref, k_ref, vaugt_ref, o_ref):
        self.i, self.h = i, h
        self.key_bias, self.query_bias = key_bias, query_bias
        self.q_ref, self.k_ref, self.vaugt_ref, self.o_ref = q_ref, k_ref, vaugt_ref, o_ref
        self.qh = self.m = self.acc = None
        first = list(range(max(0, i + 1 - STEP_TILES), i + 1))
        rest = list(range(first[0]))
        self.todo = [first] + [rest[a:a + STEP_TILES] for a in range(0, len(rest), STEP_TILES)]
        self.queue = []


def _online_scores(u, js):
    T = ATT_TILE
    if u.qh is None:
        q = u.q_ref[0, 0, u.i * T:(u.i + 1) * T, :]
        u.qh = jnp.where(_own_lanes(u.h, q.shape), q, jnp.zeros_like(q))
    s = lax.dot_general(u.k_ref[0, 0, js[0] * T:(js[-1] + 1) * T, :], u.qh, NT_DIMS,
                        preferred_element_type=f32)
    tiles, tmax = [], None
    for n, j in enumerate(js):
        sj = s[n * T:(n + 1) * T]
        kb = u.key_bias(j)
        if kb is not None:
            sj = sj + kb
        t = jnp.max(sj.reshape(T // 8, 8, T), axis=0)
        qb = u.query_bias(j)
        if qb is not None:
            t = t + qb
        tiles.append((sj, qb))
        tmax = t if tmax is None else jnp.maximum(tmax, t)
    tmax = jnp.max(tmax, axis=0, keepdims=True)
    m_new = tmax if u.m is None else jnp.maximum(u.m, tmax)
    ps = [jnp.exp2(sj - (m_new if qb is None else m_new - qb)).astype(bf16) for sj, qb in tiles]
    p = ps[0] if len(ps) == 1 else jnp.concatenate(ps, axis=0)
    alpha = None if u.m is None else jnp.exp2(u.m - m_new)
    u.m = m_new
    u.queue.append((js, p, alpha))


def _online_value(u):
    T = ATT_TILE
    js, p, alpha = u.queue.pop(0)
    pv = jnp.dot(u.vaugt_ref[u.h, :, js[0] * T:(js[-1] + 1) * T], p,
                 preferred_element_type=f32)
    u.acc = pv if alpha is None else u.acc * alpha + pv


def _run_rows(units):
    T = ATT_TILE
    done = {}
    pending = list(units)
    active = []

    def finish(u):
        heads = done.setdefault((id(u.o_ref), u.i), {})
        heads[u.h] = u.acc[0:HEAD_DIM] / u.acc[HEAD_DIM:HEAD_DIM + 1, :]
        if len(heads) == HEADS_PER_GROUP:
            out_t = jnp.concatenate([heads[h] for h in range(HEADS_PER_GROUP)], axis=0)
            u.o_ref[0, 0, u.i * T:(u.i + 1) * T, :] = out_t.T.astype(u.o_ref.dtype)
            del done[(id(u.o_ref), u.i)]

    while pending or active:
        while len(active) < ONLINE_WINDOW and pending:
            active.append(pending.pop(0))
        lagging = [u for u in active if len(u.queue) >= VALUE_LAG or (u.queue and not u.todo)]
        for u in active:
            if u.todo:
                _online_scores(u, u.todo.pop(0))
        for u in lagging:
            _online_value(u)
        for u in list(active):
            if not u.todo and not u.queue:
                active.remove(u)
                finish(u)


def _dilated_units(periodic_offsets, q_ref, k_ref, slab_ref, o_ref, vaugt_ref):
    T = ATT_TILE
    n_t = q_ref.shape[2] // T

    def key_bias(i, j):
        row0 = (n_t - 1 - i + j) * T
        if i - j in periodic_offsets:
            return jnp.tile(slab_ref[row0:row0 + SLAB_PERIOD, :], (T // SLAB_PERIOD, 1))
        return slab_ref[row0:row0 + T, :]

    return [_Rows(i, h, lambda j, i=i: key_bias(i, j), lambda j: None,
                  q_ref, k_ref, vaugt_ref, o_ref)
            for i in range(n_t) for h in range(HEADS_PER_GROUP)]


def _moba_units(q_ref, k_ref, o_ref, vaugt_ref, kmean_ref):
    T = ATT_TILE
    S = q_ref.shape[2]
    n_t = S // T

    kmean_ref[...] = jnp.zeros_like(kmean_ref)
    for j in range(n_t):
        kj = k_ref[0, 0, j * T:(j + 1) * T, :]
        kmean_ref[j:j + 1, :] = jnp.sum(kj.astype(f32), axis=0, keepdims=True) * (1.0 / T)
    kmean = kmean_ref[...]
    km_parts = []
    for h in range(HEADS_PER_GROUP):
        km_parts += list(_split_bf16(jnp.where(_own_lanes(h, kmean.shape), kmean, 0.0)))
    km_rows = kmean.shape[0]
    c0 = min((MOBA_TOPK + 1) * T, S - T)
    W = S - c0
    gates = lax.dot_general(jnp.concatenate(km_parts, axis=0), q_ref[0, 0, c0:S, :], NT_DIMS,
                            preferred_element_type=f32)

    key_i = lax.broadcasted_iota(jnp.int32, (T, T), 0)
    qry_i = lax.broadcasted_iota(jnp.int32, (T, T), 1)
    causal_bias = jnp.where(key_i <= qry_i, 0.0, NEG_INF).astype(f32)

    blk = lax.broadcasted_iota(jnp.int32, (8, W), 0)
    own_blk = (lax.broadcasted_iota(jnp.int32, (8, W), 1) + c0) // T
    past = blk < own_blk
    drop = []
    for h in range(HEADS_PER_GROUP):
        hi0 = 2 * h * km_rows
        gate = gates[hi0:hi0 + 8] + gates[hi0 + km_rows:hi0 + km_rows + 8]
        cnt = jnp.zeros((8, W), f32)
        for jp in range(n_t - 1):
            gj = gate[jp:jp + 1, :]
            ahead = (gj > gate) | ((gj == gate) & (jp < blk))
            cnt = cnt + jnp.where(ahead & (jp < own_blk), 1.0, 0.0)
        drop.append(jnp.where(past & (cnt >= float(MOBA_TOPK)), NEG_INF, 0.0))

    def query_bias(i, h, j):
        if j == i or i * T < c0:
            return None
        return drop[h][j:j + 1, i * T - c0:(i + 1) * T - c0]

    return [_Rows(i, h,
                  lambda j, i=i: causal_bias if j == i else None,
                  lambda j, i=i, h=h: query_bias(i, h, j),
                  q_ref, k_ref, vaugt_ref, o_ref)
            for i in range(n_t) for h in range(HEADS_PER_GROUP)]


def _mixers_kernel(periodic_offsets, qa_ref, ka_ref, va_ref, qb_ref, kb_ref, vb_ref, slab_ref,
                   oa_ref, ob_ref, vaugt_a_ref, vaugt_b_ref, kmean_ref):
    _fill_v_aug_t(va_ref, vaugt_a_ref)
    _fill_v_aug_t(vb_ref, vaugt_b_ref)
    dilated = _dilated_units(periodic_offsets, qa_ref, ka_ref, slab_ref, oa_ref, vaugt_a_ref)
    moba = _moba_units(qb_ref, kb_ref, ob_ref, vaugt_b_ref, kmean_ref)
    _run_rows([u for pair in zip(dilated, moba) for u in pair])


def _dilated_bias_slab(n_t):
    T = ATT_TILE
    key = np.arange(T)[:, None]
    qry = np.arange(T)[None, :]
    slab = np.empty((n_t * T, T), np.float32)
    for b in range(n_t):
        dist = (n_t - 1 - b) * T + qry - key
        mult = np.zeros((T, T), np.float64)
        for window, dil in DILATED_PATTERNS:
            mult += (dist >= 0) & (dist <= window) & (dist % dil == 0)
        slab[b * T:(b + 1) * T, :] = np.where(mult > 0, np.log2(np.maximum(mult, 1.0)), NEG_INF)
    periodic = frozenset(
        n_t - 1 - b for b in range(n_t)
        if np.array_equal(slab[b * T:(b + 1) * T],
                          np.tile(slab[b * T:b * T + SLAB_PERIOD], (T // SLAB_PERIOD, 1))))
    return jnp.asarray(slab), periodic


def _mixers(qa, ka, va, qb, kb, vb):
    B, G, S, _ = qa.shape
    assert ATT_TILE == MOBA_BLOCK and S // ATT_TILE <= 8
    slab, periodic = _dilated_bias_slab(S // ATT_TILE)
    blk = pl.BlockSpec((1, 1, S, LANES), lambda b, g: (b, g, 0, 0))
    vaugt = pltpu.VMEM((HEADS_PER_GROUP, VAUG_ROWS, S), bf16)
    out = jax.ShapeDtypeStruct(qa.shape, bf16)
    return pl.pallas_call(
        functools.partial(_mixers_kernel, periodic),
        grid=(B, G),
        in_specs=[blk] * 6 + [pl.BlockSpec(slab.shape, lambda b, g: (0, 0))],
        out_specs=[blk, blk],
        out_shape=[out, out],
        scratch_shapes=[vaugt, vaugt,
                        pltpu.VMEM((16, LANES), f32)],
        compiler_params=pltpu.CompilerParams(
            dimension_semantics=("arbitrary", "arbitrary"), vmem_limit_bytes=VMEM_LIMIT),
        name="mixers_attn",
    )(qa, ka, va, qb, kb, vb, slab)


def _out_kernel(oa_ref, ob_ref, ga_ref, x_ref, mod_ref, wgb_ref, w_ref, lng_ref, lnb_ref,
                o_ref, og_ref):
    tm = x_ref.shape[1]
    shift, scale, gate = mod_ref[0, 0:1, :], mod_ref[0, 1:2, :], mod_ref[0, 2:3, :]

    def gate_stage(rows):
        h = (x_ref[0, rows, :] * (1.0 + scale) + shift).astype(bf16)
        g_b = jnp.dot(h, wgb_ref[...], preferred_element_type=f32)
        for br, (src, g) in enumerate(((oa_ref, ga_ref[0, rows, :].astype(f32)), (ob_ref, g_b))):
            half_g = 0.5 * g
            silu = half_g + half_g * jnp.tanh(half_g)
            for gi in range(N_GROUPS):
                lo = br * D_BRANCH + gi * LANES
                og_ref[rows, lo:lo + LANES] = (src[0, gi, rows, :].astype(f32)
                                               * silu[:, gi * LANES:(gi + 1) * LANES]).astype(bf16)

    def out_stage(rows):
        y = jnp.dot(og_ref[rows, :], w_ref[...], preferred_element_type=f32)
        z = DEEPNORM_ALPHA * x_ref[0, rows, :] + gate * y
        mu = jnp.mean(z, axis=-1, keepdims=True)
        zc = z - mu
        var = jnp.mean(zc * zc, axis=-1, keepdims=True)
        o_ref[0, rows, :] = zc * lax.rsqrt(var + LN_EPS) * lng_ref[...] + lnb_ref[...]

    subs = [slice(r, r + OUT_SUB_ROWS) for r in range(0, tm, OUT_SUB_ROWS)]
    for idx in range(len(subs) + 1):
        if idx < len(subs):
            gate_stage(subs[idx])
        if idx >= 1:
            out_stage(subs[idx - 1])


def _out_projection(o_a, o_b, g_a, x, mod3, w_in_bf, w_out_bf, ln_g, ln_b):
    B, S, D = x.shape
    tm = OUT_ROWS
    att_spec = pl.BlockSpec((1, N_GROUPS, tm, LANES), lambda b, s: (b, 0, s, 0))
    row_spec = pl.BlockSpec((1, tm, D), lambda b, s: (b, s, 0))
    vec_spec = pl.BlockSpec((1, D), lambda b, s: (0, 0))
    return pl.pallas_call(
        _out_kernel,
        grid=(B, S // tm),
        in_specs=[att_spec, att_spec,
                  pl.BlockSpec((1, tm, D_BRANCH), lambda b, s: (b, s, 0)),
                  row_spec,
                  pl.BlockSpec((1, 3, D), lambda b, s: (b, 0, 0)),
                  pl.BlockSpec((D, D_BRANCH), lambda b, s: (0, 7)),
                  pl.BlockSpec(w_out_bf.shape, lambda b, s: (0, 0)),
                  vec_spec, vec_spec],
        out_specs=row_spec,
        out_shape=jax.ShapeDtypeStruct((B, S, D), f32),
        scratch_shapes=[pltpu.VMEM((tm, 2 * D_BRANCH), bf16)],
        compiler_params=pltpu.CompilerParams(
            dimension_semantics=("arbitrary", "arbitrary"), vmem_limit_bytes=VMEM_LIMIT),
        name="out_proj_ln",
    )(o_a, o_b, g_a, x, mod3, w_in_bf, w_out_bf, ln_g.reshape(1, D), ln_b.reshape(1, D))


def _rope_tables(S):
    half = HEAD_DIM // 2
    inv = ROPE_THETA ** (-jnp.arange(half, dtype=f32) / half)
    ang = jnp.arange(S, dtype=jnp.int32).astype(f32)[:, None] * inv[None, :]
    cos, sin = jnp.cos(ang), jnp.sin(ang)
    cos_head = jnp.concatenate([cos, cos], axis=-1)
    sin_head = jnp.concatenate([-sin, sin], axis=-1)
    return (jnp.tile(cos_head, (1, HEADS_PER_GROUP)), jnp.tile(sin_head, (1, HEADS_PER_GROUP)))


def kernel(x, c, w_in, w_out, w_ada, b_ada, ln_g, ln_b):
    B, S, D = x.shape
    cos_t, sin_t = _rope_tables(S)
    for layer in range(w_in.shape[0]):
        mod3 = _modulation(c, w_ada[layer], b_ada[layer]).reshape(B, 3, D)
        w_in_bf = w_in[layer].astype(bf16)
        qa, ka, va, g_a, qb, kb, vb = _projection(x, mod3, cos_t, sin_t, w_in_bf)
        o_a, o_b = _mixers(qa, ka, va, qb, kb, vb)
        x = _out_projection(o_a, o_b, g_a, x, mod3, w_in_bf, w_out[layer].astype(bf16),
                            ln_g[layer], ln_b[layer])
    return x
```

```python
import functools

import numpy as np
import jax
import jax.numpy as jnp
from jax import lax
from jax.experimental import pallas as pl
from jax.experimental.pallas import tpu as pltpu

HEAD_DIM = 64
N_HEADS = 8
D_BRANCH = N_HEADS * HEAD_DIM
DILATED_PATTERNS = ((128, 1), (512, 4), (2048, 16))
MOBA_BLOCK = 256
MOBA_TOPK = 3
ROPE_THETA = 10000.0
LN_EPS = 1e-5
NEG_INF = -1e30
DEPTH = 1
DEEPNORM_ALPHA = (2.0 * DEPTH) ** 0.25
LOG2E = 1.4426950408889634
Q_SCALE = HEAD_DIM ** -0.5 * LOG2E

LANES = 128
HEADS_PER_GROUP = LANES // HEAD_DIM
N_GROUPS = N_HEADS // HEADS_PER_GROUP
ATT_TILE = 256
ONLINE_WINDOW = 3
STEP_TILES = 1
VALUE_LAG = 4
SLAB_PERIOD = 16
PROJ_ROWS = 1024
OUT_ROWS = 1024
OUT_SUB_ROWS = 256
VMEM_LIMIT = 48 * 1024 * 1024

NT_DIMS = (((1,), (1,)), ((), ()))

f32 = jnp.float32
bf16 = jnp.bfloat16


def _split_bf16(a):
    hi = a.astype(bf16)
    return hi, (a - hi.astype(f32)).astype(bf16)


def _mod_kernel(c_ref, w_ref, b_ref, o_ref):
    c_hi, c_lo = _split_bf16(c_ref[...])
    w_hi, w_lo = _split_bf16(w_ref[...])
    dot = lambda a, b: jnp.dot(a, b, preferred_element_type=f32)
    o_ref[...] = dot(c_hi, w_hi) + (dot(c_hi, w_lo) + dot(c_lo, w_hi)) + b_ref[...]


def _modulation(c, w_ada, b_ada):
    B, D = c.shape
    N = w_ada.shape[1]
    bn = 512
    return pl.pallas_call(
        _mod_kernel,
        grid=(N // bn,),
        in_specs=[pl.BlockSpec((B, D), lambda n: (0, 0)),
                  pl.BlockSpec((D, bn), lambda n: (0, n)),
                  pl.BlockSpec((1, bn), lambda n: (0, n))],
        out_specs=pl.BlockSpec((B, bn), lambda n: (0, n)),
        out_shape=jax.ShapeDtypeStruct((B, N), f32),
        name="adaln_mod",
    )(c, w_ada, b_ada.reshape(1, N))


def _modulated(x_ref, mod_ref):
    shift = mod_ref[0, 0:1, :]
    scale = mod_ref[0, 1:2, :]
    return (x_ref[0] * (1.0 + scale) + shift).astype(bf16)


def _proj_kernel(x_ref, mod_ref, cos_ref, sin_ref, w_ref,
                 qa_ref, ka_ref, va_ref, ga_ref, qb_ref, kb_ref, vb_ref):
    tm = x_ref.shape[1]
    h = _modulated(x_ref, mod_ref)
    cos = cos_ref[...]
    sin = sin_ref[...]
    lane = lax.broadcasted_iota(jnp.int32, (tm, LANES), 1)
    first_half = (lane & (HEAD_DIM // 2)) == 0

    def rope(t):
        partner = jnp.where(first_half,
                            pltpu.roll(t, LANES - HEAD_DIM // 2, 1),
                            pltpu.roll(t, HEAD_DIM // 2, 1))
        return t * cos + partner * sin

    targets = (qa_ref, ka_ref, va_ref, ga_ref, qb_ref, kb_ref, vb_ref)
    kinds = ("q", "k", "v", "g", "q", "k", "v")
    for c in range(len(targets)):
        acc = jnp.dot(h, w_ref[:, c * D_BRANCH:(c + 1) * D_BRANCH],
                      preferred_element_type=f32)
        if kinds[c] == "g":
            ga_ref[0] = acc.astype(bf16)
            continue
        for gi in range(N_GROUPS):
            t = acc[:, gi * LANES:(gi + 1) * LANES]
            if kinds[c] == "q":
                t = rope(t) * Q_SCALE
            elif kinds[c] == "k":
                t = rope(t)
            targets[c][0, gi] = t.astype(bf16)


def _projection(x, mod3, cos_t, sin_t, w_in_bf):
    B, S, D = x.shape
    tm = PROJ_ROWS
    n_s = S // tm
    qkv_shape = jax.ShapeDtypeStruct((B, N_GROUPS, S, LANES), bf16)
    qkv_spec = pl.BlockSpec((1, N_GROUPS, tm, LANES), lambda b, s: (b, 0, s, 0))
    g_shape = jax.ShapeDtypeStruct((B, S, D_BRANCH), bf16)
    g_spec = pl.BlockSpec((1, tm, D_BRANCH), lambda b, s: (b, s, 0))
    return pl.pallas_call(
        _proj_kernel,
        grid=(B, n_s),
        in_specs=[pl.BlockSpec((1, tm, D), lambda b, s: (b, s, 0)),
                  pl.BlockSpec((1, 3, D), lambda b, s: (b, 0, 0)),
                  pl.BlockSpec((tm, LANES), lambda b, s: (s, 0)),
                  pl.BlockSpec((tm, LANES), lambda b, s: (s, 0)),
                  pl.BlockSpec((D, 7 * D_BRANCH), lambda b, s: (0, 0))],
        out_specs=[qkv_spec] * 3 + [g_spec] + [qkv_spec] * 3,
        out_shape=[qkv_shape] * 3 + [g_shape] + [qkv_shape] * 3,
        compiler_params=pltpu.CompilerParams(
            dimension_semantics=("arbitrary", "arbitrary"), vmem_limit_bytes=VMEM_LIMIT),
        name="in_proj_rope",
    )(x, mod3, cos_t, sin_t, w_in_bf)


def _own_lanes(h, shape):
    lane = lax.broadcasted_iota(jnp.int32, shape, len(shape) - 1)
    return (lane < HEAD_DIM) if h == 0 else (lane >= HEAD_DIM)


def _own_rows(h, shape):
    row = lax.broadcasted_iota(jnp.int32, shape, 0)
    return (row < HEAD_DIM) if h == 0 else (row >= HEAD_DIM)


def _fill_v_aug_t(v_ref, vaugt_ref):
    vt = v_ref[0, 0].astype(f32).T
    for h in range(HEADS_PER_GROUP):
        vaugt_ref[h] = jnp.where(_own_rows(h, vt.shape), vt, 1.0).astype(bf16)


class _Rows:
    def __init__(self, i, h, key_bias, query_bias, q_ref, k_ref, vaugt_ref, o_ref):
        self.i, self.h = i, h
        self.key_bias, self.query_bias = key_bias, query_bias
        self.q_ref, self.k_ref, self.vaugt_ref, self.o_ref = q_ref, k_ref, vaugt_ref, o_ref
        self.qh = self.m = self.acc = None
        first = list(range(max(0, i + 1 - STEP_TILES), i + 1))
        rest = list(range(first[0]))
        self.todo = [first] + [rest[a:a + STEP_TILES] for a in range(0, len(rest), STEP_TILES)]
        self.queue = []


def _online_scores(u, js):
    T = ATT_TILE
    if u.qh is None:
        q = u.q_ref[0, 0, u.i * T:(u.i + 1) * T, :]
        u.qh = jnp.where(_own_lanes(u.h, q.shape), q, jnp.zeros_like(q))
    s = lax.dot_general(u.k_ref[0, 0, js[0] * T:(js[-1] + 1) * T, :], u.qh, NT_DIMS,
                        preferred_element_type=f32)
    tiles, tmax = [], None
    for n, j in enumerate(js):
        sj = s[n * T:(n + 1) * T]
        kb = u.key_bias(j)
        if kb is not None:
            sj = sj + kb
        t = jnp.max(sj.reshape(T // 8, 8, T), axis=0)
        qb = u.query_bias(j)
        if qb is not None:
            t = t + qb
        tiles.append((sj, qb))
        tmax = t if tmax is None else jnp.maximum(tmax, t)
    tmax = jnp.max(tmax, axis=0, keepdims=True)
    m_new = tmax if u.m is None else jnp.maximum(u.m, tmax)
    ps = [jnp.exp2(sj - (m_new if qb is None else m_new - qb)).astype(bf16) for sj, qb in tiles]
    p = ps[0] if len(ps) == 1 else jnp.concatenate(ps, axis=0)
    alpha = None if u.m is None else jnp.exp2(u.m - m_new)
    u.m = m_new
    u.queue.append((js, p, alpha))


def _online_value(u):
    T = ATT_TILE
    js, p, alpha = u.queue.pop(0)
    pv = jnp.dot(u.vaugt_ref[u.h, :, js[0] * T:(js[-1] + 1) * T], p,
                 preferred_element_type=f32)
    u.acc = pv if alpha is None else u.acc * alpha + pv


def _run_rows(units):
    T = ATT_TILE
    done = {}
    pending = list(units)
    active = []

    def finish(u):
        den_row = HEAD_DIM if u.h == 0 else 0
        heads = done.setdefault((id(u.o_ref), u.i), {})
        heads[u.h] = u.acc / u.acc[den_row:den_row + 1, :]
        if len(heads) == HEADS_PER_GROUP:
            out_t = jnp.where(_own_rows(0, heads[0].shape), heads[0], heads[1])
            u.o_ref[0, 0, u.i * T:(u.i + 1) * T, :] = out_t.T.astype(u.o_ref.dtype)
            del done[(id(u.o_ref), u.i)]

    while pending or active:
        while len(active) < ONLINE_WINDOW and pending:
            active.append(pending.pop(0))
        lagging = [u for u in active if len(u.queue) >= VALUE_LAG or (u.queue and not u.todo)]
        for u in active:
            if u.todo:
                _online_scores(u, u.todo.pop(0))
        for u in lagging:
            _online_value(u)
        for u in list(active):
            if not u.todo and not u.queue:
                active.remove(u)
                finish(u)


def _dilated_units(periodic_offsets, q_ref, k_ref, slab_ref, o_ref, vaugt_ref):
    T = ATT_TILE
    n_t = q_ref.shape[2] // T

    def key_bias(i, j):
        row0 = (n_t - 1 - i + j) * T
        if i - j in periodic_offsets:
            return jnp.tile(slab_ref[row0:row0 + SLAB_PERIOD, :], (T // SLAB_PERIOD, 1))
        return slab_ref[row0:row0 + T, :]

    return [_Rows(i, h, lambda j, i=i: key_bias(i, j), lambda j: None,
                  q_ref, k_ref, vaugt_ref, o_ref)
            for i in range(n_t) for h in range(HEADS_PER_GROUP)]


def _moba_units(q_ref, k_ref, o_ref, vaugt_ref, kmean_ref):
    T = ATT_TILE
    S = q_ref.shape[2]
    n_t = S // T

    kmean_ref[...] = jnp.zeros_like(kmean_ref)
    for j in range(n_t):
        kj = k_ref[0, 0, j * T:(j + 1) * T, :]
        kmean_ref[j:j + 1, :] = jnp.sum(kj.astype(f32), axis=0, keepdims=True) * (1.0 / T)
    kmean = kmean_ref[...]
    km_parts = []
    for h in range(HEADS_PER_GROUP):
        km_parts += list(_split_bf16(jnp.where(_own_lanes(h, kmean.shape), kmean, 0.0)))
    km_rows = kmean.shape[0]
    c0 = min((MOBA_TOPK + 1) * T, S - T)
    W = S - c0
    gates = lax.dot_general(jnp.concatenate(km_parts, axis=0), q_ref[0, 0, c0:S, :], NT_DIMS,
                            preferred_element_type=f32)

    key_i = lax.broadcasted_iota(jnp.int32, (T, T), 0)
    qry_i = lax.broadcasted_iota(jnp.int32, (T, T), 1)
    causal_bias = jnp.where(key_i <= qry_i, 0.0, NEG_INF).astype(f32)

    blk = lax.broadcasted_iota(jnp.int32, (8, W), 0)
    own_blk = (lax.broadcasted_iota(jnp.int32, (8, W), 1) + c0) // T
    past = blk < own_blk
    drop = []
    for h in range(HEADS_PER_GROUP):
        hi0 = 2 * h * km_rows
        gate = gates[hi0:hi0 + 8] + gates[hi0 + km_rows:hi0 + km_rows + 8]
        cnt = jnp.zeros((8, W), f32)
        for jp in range(n_t - 1):
            gj = gate[jp:jp + 1, :]
            ahead = (gj > gate) | ((gj == gate) & (jp < blk))
            cnt = cnt + jnp.where(ahead & (jp < own_blk), 1.0, 0.0)
        drop.append(jnp.where(past & (cnt >= float(MOBA_TOPK)), NEG_INF, 0.0))

    def query_bias(i, h, j):
        if j == i or i * T < c0:
            return None
        return drop[h][j:j + 1, i * T - c0:(i + 1) * T - c0]

    return [_Rows(i, h,
                  lambda j, i=i: causal_bias if j == i else None,
                  lambda j, i=i, h=h: query_bias(i, h, j),
                  q_ref, k_ref, vaugt_ref, o_ref)
            for i in range(n_t) for h in range(HEADS_PER_GROUP)]


def _mixers_kernel(periodic_offsets, qa_ref, ka_ref, va_ref, qb_ref, kb_ref, vb_ref, slab_ref,
                   oa_ref, ob_ref, vaugt_a_ref, vaugt_b_ref, kmean_ref):
    _fill_v_aug_t(va_ref, vaugt_a_ref)
    _fill_v_aug_t(vb_ref, vaugt_b_ref)
    dilated = _dilated_units(periodic_offsets, qa_ref, ka_ref, slab_ref, oa_ref, vaugt_a_ref)
    moba = _moba_units(qb_ref, kb_ref, ob_ref, vaugt_b_ref, kmean_ref)
    _run_rows([u for pair in zip(dilated, moba) for u in pair])


def _dilated_bias_slab(n_t):
    T = ATT_TILE
    key = np.arange(T)[:, None]
    qry = np.arange(T)[None, :]
    slab = np.empty((n_t * T, T), np.float32)
    for b in range(n_t):
        dist = (n_t - 1 - b) * T + qry - key
        mult = np.zeros((T, T), np.float64)
        for window, dil in DILATED_PATTERNS:
            mult += (dist >= 0) & (dist <= window) & (dist % dil == 0)
        slab[b * T:(b + 1) * T, :] = np.where(mult > 0, np.log2(np.maximum(mult, 1.0)), NEG_INF)
    periodic = frozenset(
        n_t - 1 - b for b in range(n_t)
        if np.array_equal(slab[b * T:(b + 1) * T],
                          np.tile(slab[b * T:b * T + SLAB_PERIOD], (T // SLAB_PERIOD, 1))))
    return jnp.asarray(slab), periodic


def _mixers(qa, ka, va, qb, kb, vb):
    B, G, S, _ = qa.shape
    assert ATT_TILE == MOBA_BLOCK and S // ATT_TILE <= 8
    slab, periodic = _dilated_bias_slab(S // ATT_TILE)
    blk = pl.BlockSpec((1, 1, S, LANES), lambda b, g: (b, g, 0, 0))
    vaugt = pltpu.VMEM((HEADS_PER_GROUP, LANES, S), bf16)
    out = jax.ShapeDtypeStruct(qa.shape, bf16)
    return pl.pallas_call(
        functools.partial(_mixers_kernel, periodic),
        grid=(B, G),
        in_specs=[blk] * 6 + [pl.BlockSpec(slab.shape, lambda b, g: (0, 0))],
        out_specs=[blk, blk],
        out_shape=[out, out],
        scratch_shapes=[vaugt, vaugt,
                        pltpu.VMEM((16, LANES), f32)],
        compiler_params=pltpu.CompilerParams(
            dimension_semantics=("arbitrary", "arbitrary"), vmem_limit_bytes=VMEM_LIMIT),
        name="mixers_attn",
    )(qa, ka, va, qb, kb, vb, slab)


def _out_kernel(oa_ref, ob_ref, ga_ref, x_ref, mod_ref, wgb_ref, w_ref, lng_ref, lnb_ref,
                o_ref, og_ref):
    tm = x_ref.shape[1]
    shift, scale, gate = mod_ref[0, 0:1, :], mod_ref[0, 1:2, :], mod_ref[0, 2:3, :]

    def gate_stage(rows):
        h = (x_ref[0, rows, :] * (1.0 + scale) + shift).astype(bf16)
        g_b = jnp.dot(h, wgb_ref[...], preferred_element_type=f32)
        for br, (src, g) in enumerate(((oa_ref, ga_ref[0, rows, :].astype(f32)), (ob_ref, g_b))):
            half_g = 0.5 * g
            silu = half_g + half_g * jnp.tanh(half_g)
            for gi in range(N_GROUPS):
                lo = br * D_BRANCH + gi * LANES
                og_ref[rows, lo:lo + LANES] = (src[0, gi, rows, :].astype(f32)
                                               * silu[:, gi * LANES:(gi + 1) * LANES]).astype(bf16)

    def out_stage(rows):
        y = jnp.dot(og_ref[rows, :], w_ref[...], preferred_element_type=f32)
        z = DEEPNORM_ALPHA * x_ref[0, rows, :] + gate * y
        mu = jnp.mean(z, axis=-1, keepdims=True)
        zc = z - mu
        var = jnp.mean(zc * zc, axis=-1, keepdims=True)
        o_ref[0, rows, :] = zc * lax.rsqrt(var + LN_EPS) * lng_ref[...] + lnb_ref[...]

    subs = [slice(r, r + OUT_SUB_ROWS) for r in range(0, tm, OUT_SUB_ROWS)]
    for idx in range(len(subs) + 1):
        if idx < len(subs):
            gate_stage(subs[idx])
        if idx >= 1:
            out_stage(subs[idx - 1])


def _out_projection(o_a, o_b, g_a, x, mod3, w_in_bf, w_out_bf, ln_g, ln_b):
    B, S, D = x.shape
    tm = OUT_ROWS
    att_spec = pl.BlockSpec((1, N_GROUPS, tm, LANES), lambda b, s: (b, 0, s, 0))
    row_spec = pl.BlockSpec((1, tm, D), lambda b, s: (b, s, 0))
    vec_spec = pl.BlockSpec((1, D), lambda b, s: (0, 0))
    return pl.pallas_call(
        _out_kernel,
        grid=(B, S // tm),
        in_specs=[att_spec, att_spec,
                  pl.BlockSpec((1, tm, D_BRANCH), lambda b, s: (b, s, 0)),
                  row_spec,
                  pl.BlockSpec((1, 3, D), lambda b, s: (b, 0, 0)),
                  pl.BlockSpec((D, D_BRANCH), lambda b, s: (0, 7)),
                  pl.BlockSpec(w_out_bf.shape, lambda b, s: (0, 0)),
                  vec_spec, vec_spec],
        out_specs=row_spec,
        out_shape=jax.ShapeDtypeStruct((B, S, D), f32),
        scratch_shapes=[pltpu.VMEM((tm, 2 * D_BRANCH), bf16)],
        compiler_params=pltpu.CompilerParams(
            dimension_semantics=("arbitrary", "arbitrary"), vmem_limit_bytes=VMEM_LIMIT),
        name="out_proj_ln",
    )(o_a, o_b, g_a, x, mod3, w_in_bf, w_out_bf, ln_g.reshape(1, D), ln_b.reshape(1, D))


def _rope_tables(S):
    half = HEAD_DIM // 2
    inv = ROPE_THETA ** (-jnp.arange(half, dtype=f32) / half)
    ang = jnp.arange(S, dtype=jnp.int32).astype(f32)[:, None] * inv[None, :]
    cos, sin = jnp.cos(ang), jnp.sin(ang)
    cos_head = jnp.concatenate([cos, cos], axis=-1)
    sin_head = jnp.concatenate([-sin, sin], axis=-1)
    return (jnp.tile(cos_head, (1, HEADS_PER_GROUP)), jnp.tile(sin_head, (1, HEADS_PER_GROUP)))


def kernel(x, c, w_in, w_out, w_ada, b_ada, ln_g, ln_b):
    B, S, D = x.shape
    cos_t, sin_t = _rope_tables(S)
    for layer in range(w_in.shape[0]):
        mod3 = _modulation(c, w_ada[layer], b_ada[layer]).reshape(B, 3, D)
        w_in_bf = w_in[layer].astype(bf16)
        qa, ka, va, g_a, qb, kb, vb = _projection(x, mod3, cos_t, sin_t, w_in_bf)
        o_a, o_b = _mixers(qa, ka, va, qb, kb, vb)
        x = _out_projection(o_a, o_b, g_a, x, mod3, w_in_bf, w_out[layer].astype(bf16),
                            ln_g[layer], ln_b[layer])
    return x
```
